```python
import math
import jax, jax.numpy as jnp
from jax import lax
import numpy as np

D_MODEL = 2048
BATCH = 4
SEQ = 2048
DEPTH = 4
DEC_BATCH = 8
DEC_SEQ = 1
PAST_LEN = 16384
PAGE_SIZE = 128

N_META = 16
D_MIX = D_MODEL
C_CONV = D_MIX // 2
D_ATT = D_MIX - C_CONV
N_HEADS = 8
HEAD_DV = D_ATT // N_HEADS
HEAD_DK = HEAD_DV // 2
CONV_K = 31
N_BUCKETS = 32
MAX_DIST = 128
Q_BLOCK = 128
D_IN = 3 * C_CONV + 4 * D_ATT
SPLITS = (C_CONV, 2 * C_CONV, 3 * C_CONV, 3 * C_CONV + D_ATT, 3 * C_CONV + 2 * D_ATT, 3 * C_CONV + 3 * D_ATT)
SCALE = HEAD_DK ** -0.5
NEG = -1e30

kernel_name = "hymba_conformer_diffattn_step"


def rms_norm(x, w, eps=1e-6):
    x32 = x.astype(jnp.float32)
    y = x32 * lax.rsqrt(jnp.mean(x32 * x32, axis=-1, keepdims=True) + eps)
    return y.astype(x.dtype) * w


def layer_norm(x, w, b, eps=1e-5):
    x32 = x.astype(jnp.float32)
    mu = jnp.mean(x32, axis=-1, keepdims=True)
    var = jnp.mean(jnp.square(x32 - mu), axis=-1, keepdims=True)
    return ((x32 - mu) * lax.rsqrt(var + eps)).astype(x.dtype) * w + b


def rel_bucket(rel):
    n = jnp.maximum(rel, 0)
    max_exact = N_BUCKETS // 2
    nf = jnp.maximum(n, 1).astype(jnp.float32)
    large = max_exact + (jnp.log(nf / max_exact) / math.log(MAX_DIST / max_exact)
                         * (N_BUCKETS - max_exact)).astype(jnp.int32)
    large = jnp.minimum(large, N_BUCKETS - 1)
    return jnp.where(n < max_exact, n, large)


def position_bias(q_pos, k_pos, rel_bias):
    b = rel_bias[rel_bucket(q_pos[:, None] - k_pos[None, :])]
    return jnp.transpose(b, (2, 0, 1)).astype(jnp.float32)


def diff_attend(q, k, v, q_pos, k_pos, rel_bias, lam):
    s = jnp.einsum('nqhcd,nkhcd->nchqk', q, k, preferred_element_type=jnp.float32) * SCALE
    s = s + position_bias(q_pos, k_pos, rel_bias)[None, None]
    s = jnp.where((k_pos[None, :] <= q_pos[:, None])[None, None, None], s, NEG)
    p = jax.nn.softmax(s, axis=-1)
    wgt = p[:, 0] - lam * p[:, 1]
    return jnp.einsum('nhqk,nkhd->nqhd', wgt.astype(v.dtype), v)


def prompt_attention(q, k, v, rel_bias, lam):
    n, L = q.shape[0], q.shape[1]
    pos = jnp.arange(L)
    o_meta = diff_attend(q[:, :N_META], k[:, :N_META], v[:, :N_META],
                         pos[:N_META], pos[:N_META], rel_bias, lam)
    nb = (L - N_META) // Q_BLOCK
    qb = jnp.swapaxes(q[:, N_META:].reshape(n, nb, Q_BLOCK, N_HEADS, 2, HEAD_DK), 0, 1)
    pb = pos[N_META:].reshape(nb, Q_BLOCK)
    ob = lax.map(lambda a: diff_attend(a[0], k, v, a[1], pos, rel_bias, lam), (qb, pb))
    ob = jnp.swapaxes(ob, 0, 1).reshape(n, nb * Q_BLOCK, N_HEADS, HEAD_DV)
    return jnp.concatenate([o_meta, ob], axis=1)


def conv_branch(a, g, prefix, conv_w, conv_b, ln_w, ln_b, w_pw2):
    u = a * jax.nn.sigmoid(g)
    full = jnp.concatenate([prefix.astype(u.dtype), u], axis=1)
    y = lax.conv_general_dilated(full, conv_w[:, None, :].astype(u.dtype), window_strides=(1,),
                                 padding='VALID', dimension_numbers=('NWC', 'WIO', 'NWC'),
                                 feature_group_count=C_CONV) + conv_b
    y = jax.nn.silu(layer_norm(y, ln_w, ln_b))
    return y @ w_pw2, full[:, -(CONV_K - 1):]


def mixer(xn, conv_prefix, past_k, past_v, layer, rel_bias, p):
    (w_in, conv_w, conv_b, ln_w, ln_b, w_pw2, lq1, lk1, lq2, lk2, subln_w, w_out) = p
    n, t, _ = xn.shape
    h = xn @ w_in
    a, g, cg, q, k, v, ag = jnp.split(h, SPLITS, axis=-1)
    conv_y, new_conv = conv_branch(a, g, conv_prefix, conv_w, conv_b, ln_w, ln_b, w_pw2)
    conv_y = conv_y * jax.nn.silu(cg)
    q = q.reshape(n, t, N_HEADS, 2, HEAD_DK)
    k = k.reshape(n, t, N_HEADS, 2, HEAD_DK)
    v = v.reshape(n, t, N_HEADS, HEAD_DV)
    lam_init = 0.8 - 0.6 * math.exp(-0.3 * layer)
    f32 = jnp.float32
    lam = (jnp.exp(jnp.sum(lq1.astype(f32) * lk1.astype(f32)))
           - jnp.exp(jnp.sum(lq2.astype(f32) * lk2.astype(f32))) + lam_init)
    if past_k is None:
        o = prompt_attention(q, k, v, rel_bias, lam)
    else:
        plen = past_k.shape[1]
        kk = jnp.concatenate([past_k.astype(k.dtype), k], axis=1)
        vv = jnp.concatenate([past_v.astype(v.dtype), v], axis=1)
        o = diff_attend(q, kk, vv, plen + jnp.arange(t), jnp.arange(plen + t), rel_bias, lam)
    o = rms_norm(o, subln_w, eps=1e-5) * (1.0 - lam_init)
    att_y = o.reshape(n, t, D_ATT) * jax.nn.silu(ag)
    y = jnp.concatenate([conv_y, att_y], axis=-1) @ w_out
    return y, k, v, new_conv


def setup_inputs(seed: int = 0) -> dict:
    key = jax.random.key(seed)
    ks = jax.random.split(key, 24)
    f32 = jnp.float32
    n_pages = PAST_LEN // PAGE_SIZE
    n_used = DEC_BATCH * n_pages
    n_pool = n_used + max(1, n_used // 4)
    nrm = lambda k, s, sc: jax.random.normal(k, s, f32) * sc
    page_table = jax.random.permutation(ks[0], n_pool)[:n_used].reshape(DEC_BATCH, n_pages).astype(jnp.int32)
    return {
        "x_prompt": nrm(ks[1], (BATCH, SEQ, D_MODEL), 1.0),
        "x_sample": nrm(ks[2], (DEC_BATCH, DEC_SEQ, D_MODEL), 1.0),
        "cache_k": nrm(ks[3], (DEPTH, n_pool, PAGE_SIZE, N_HEADS, 2, HEAD_DK), 1.0),
        "cache_v": nrm(ks[4], (DEPTH, n_pool, PAGE_SIZE, N_HEADS, HEAD_DV), 1.0),
        "state_conv": nrm(ks[5], (DEPTH, DEC_BATCH, CONV_K - 1, C_CONV), 1.0),
        "page_table": page_table,
        "meta_tokens": nrm(ks[6], (N_META, D_MODEL), 1.0),
        "rel_bias": nrm(ks[7], (N_BUCKETS, N_HEADS), 0.5),
        "norm_w": 1.0 + nrm(ks[8], (DEPTH, D_MODEL), 0.02),
        "w_in": nrm(ks[9], (DEPTH, D_MODEL, D_IN), D_MODEL ** -0.5),
        "conv_w": nrm(ks[10], (DEPTH, CONV_K, C_CONV), CONV_K ** -0.5),
        "conv_b": nrm(ks[11], (DEPTH, C_CONV), 0.01),
        "conv_ln_w": 1.0 + nrm(ks[12], (DEPTH, C_CONV), 0.02),
        "conv_ln_b": nrm(ks[13], (DEPTH, C_CONV), 0.01),
        "w_pw2": nrm(ks[14], (DEPTH, C_CONV, C_CONV), C_CONV ** -0.5),
        "lambda_q1": nrm(ks[15], (DEPTH, HEAD_DK), 0.1),
        "lambda_k1": nrm(ks[16], (DEPTH, HEAD_DK), 0.1),
        "lambda_q2": nrm(ks[17], (DEPTH, HEAD_DK), 0.1),
        "lambda_k2": nrm(ks[18], (DEPTH, HEAD_DK), 0.1),
        "subln_w": 1.0 + nrm(ks[19], (DEPTH, HEAD_DV), 0.02),
        "w_out": nrm(ks[20], (DEPTH, D_MIX, D_MODEL), D_MIX ** -0.5),
        "final_norm_w": 1.0 + nrm(ks[21], (D_MODEL,), 0.02),
    }


def reference(x_prompt, x_sample, cache_k, cache_v, state_conv, page_table, meta_tokens, rel_bias,
              norm_w, w_in, conv_w, conv_b, conv_ln_w, conv_ln_b, w_pw2, lambda_q1, lambda_k1,
              lambda_q2, lambda_k2, subln_w, w_out, final_norm_w):
    b = x_prompt.shape[0]
    db, n_pages = page_table.shape
    plen = n_pages * cache_k.shape[2]
    meta = jnp.broadcast_to(meta_tokens[None].astype(x_prompt.dtype), (b, N_META, D_MODEL))
    hp = jnp.concatenate([meta, x_prompt], axis=1)
    hs = x_sample
    zero_prefix = jnp.zeros((b, CONV_K - 1, C_CONV), hp.dtype)
    kp_l, vp_l, cp_l, ks_l, vs_l, cs_l = [], [], [], [], [], []
    for l in range(DEPTH):
        p = (w_in[l], conv_w[l], conv_b[l], conv_ln_w[l], conv_ln_b[l], w_pw2[l],
             lambda_q1[l], lambda_k1[l], lambda_q2[l], lambda_k2[l], subln_w[l], w_out[l])
        yp, kp, vp, cp = mixer(rms_norm(hp, norm_w[l]), zero_prefix, None, None, l, rel_bias, p)
        hp = hp + yp
        past_k = cache_k[l][page_table].reshape(db, plen, N_HEADS, 2, HEAD_DK)
        past_v = cache_v[l][page_table].reshape(db, plen, N_HEADS, HEAD_DV)
        ys, ksn, vsn, csn = mixer(rms_norm(hs, norm_w[l]), state_conv[l], past_k, past_v, l, rel_bias, p)
        hs = hs + ys
        kp_l.append(kp); vp_l.append(vp); cp_l.append(cp)
        ks_l.append(ksn); vs_l.append(vsn); cs_l.append(csn)
    y_prompt = rms_norm(hp, final_norm_w)[:, N_META:]
    y_sample = rms_norm(hs, final_norm_w)
    return (y_prompt, y_sample, jnp.stack(kp_l), jnp.stack(vp_l), jnp.stack(cp_l),
            jnp.stack(ks_l), jnp.stack(vs_l), jnp.stack(cs_l))
```

```python
import functools
import math

import jax
import jax.numpy as jnp
from jax import lax
from jax.experimental import pallas as pl
from jax.experimental.pallas import tpu as pltpu

F32 = jnp.float32
BF16 = jnp.bfloat16

N_META = 16
N_HEADS = 8
HEAD_DV = 128
HEAD_DK = 64
CONV_K = 31
N_BUCKETS = 32
MAX_DIST = 128
NEG = -1e30
SIDE_ROWS = 32
HALO = 32
ROW_TILE = 256
ATT_TILE = 256
CONV_CHUNK = 32
PAGES_PER_STEP = 8
VMEM_LIMIT = 56 * 1024 * 1024


def _dot(a, b):
    return jnp.dot(a, b, preferred_element_type=F32)


def _dot_nt(a, b):
    return lax.dot_general(a, b, (((1,), (1,)), ((), ())), preferred_element_type=F32)


def _silu(x):
    return x * jax.nn.sigmoid(x)


def _resident(shape):
    return pl.BlockSpec(shape, lambda *_: (0,) * len(shape), pipeline_mode=pl.Buffered(1))


def _inproj_kernel(x_ref, nw_ref, w_ref, u_ref, cg_ref, q_ref, k_ref, v_ref, ag_ref,
                   kb_ref, vb_ref, *, c_conv, d_att, scale):
    x = x_ref[...]
    ms = jnp.mean(x * x, axis=-1, keepdims=True)
    xb = ((x * lax.rsqrt(ms + 1e-6)) * nw_ref[...]).astype(BF16)

    def mm(c0, width):
        return _dot(xb, w_ref[:, c0:c0 + width])

    a = mm(0, c_conv)
    g = mm(c_conv, c_conv)
    u_ref[...] = a * jax.nn.sigmoid(g)
    cg_ref[...] = mm(2 * c_conv, c_conv)
    base = 3 * c_conv
    q_ref[...] = (mm(base, d_att) * scale).astype(BF16)
    k = mm(base + d_att, d_att)
    k_ref[...] = k
    kb_ref[...] = k.astype(BF16)
    v = mm(base + 2 * d_att, d_att)
    v_ref[...] = v
    vb_ref[...] = v.astype(BF16)
    ag_ref[...] = mm(base + 3 * d_att, d_att)


def _inproj(x, norm_w, w_in_bf, tm):
    rows, d = x.shape
    d_in = w_in_bf.shape[1]
    c_conv = d // 2
    d_att = d - c_conv
    row_f32 = lambda w: pl.BlockSpec((tm, w), lambda i: (i, 0))
    shapes = [
        jax.ShapeDtypeStruct((rows, c_conv), F32),
        jax.ShapeDtypeStruct((rows, c_conv), F32),
        jax.ShapeDtypeStruct((rows, d_att), BF16),
        jax.ShapeDtypeStruct((rows, d_att), F32),
        jax.ShapeDtypeStruct((rows, d_att), F32),
        jax.ShapeDtypeStruct((rows, d_att), F32),
        jax.ShapeDtypeStruct((rows, d_att), BF16),
        jax.ShapeDtypeStruct((rows, d_att), BF16),
    ]
    return pl.pallas_call(
        functools.partial(_inproj_kernel, c_conv=c_conv, d_att=d_att, scale=HEAD_DK ** -0.5),
        grid=(rows // tm,),
        in_specs=[row_f32(d), _resident((1, d)), _resident((d, d_in))],
        out_specs=[row_f32(s.shape[1]) for s in shapes],
        out_shape=shapes,
        compiler_params=pltpu.CompilerParams(
            dimension_semantics=("arbitrary",), vmem_limit_bytes=VMEM_LIMIT),
        name="inproj",
    )(x, norm_w.reshape(1, d), w_in_bf)


def _flash_kernel(lam_ref, q_ref, k_ref, v_ref, km_ref, vm_ref, bmeta_ref, bsub_ref, bdiag_ref,
                  o_ref, m_sc, l_sc, acc_sc, *, tile):
    qi = pl.program_id(2)
    q = q_ref[...]
    lane = lax.broadcasted_iota(jnp.int32, q.shape, 1)
    zero = jnp.zeros_like(q)
    q2 = jnp.concatenate([jnp.where(lane < HEAD_DK, q, zero),
                          jnp.where(lane >= HEAD_DK, q, zero)], axis=0)

    s = _dot_nt(q2, km_ref[...]) + jnp.where(qi == 0, bmeta_ref[...], 0.0)
    m0 = jnp.max(s, axis=-1, keepdims=True)
    p = jnp.exp(s - m0)
    m_sc[...] = m0
    l_sc[...] = jnp.sum(p, axis=-1, keepdims=True)
    acc_sc[...] = _dot(p.astype(BF16), vm_ref[...])

    def step(s, v):
        m_prev = m_sc[...]
        m_new = jnp.maximum(m_prev, jnp.max(s, axis=-1, keepdims=True))
        alpha = jnp.exp(m_prev - m_new)
        p = jnp.exp(s - m_new)
        l_sc[...] = alpha * l_sc[...] + jnp.sum(p, axis=-1, keepdims=True)
        acc_sc[...] = alpha * acc_sc[...] + _dot(p.astype(BF16), v)
        m_sc[...] = m_new

    def kv(ki):
        r0 = pl.multiple_of(ki * tile, tile)
        return k_ref[pl.ds(r0, tile), :], v_ref[pl.ds(r0, tile), :]

    def far(ki, carry):
        k, v = kv(ki)
        step(_dot_nt(q2, k), v)
        return carry

    lax.fori_loop(0, jnp.maximum(qi - 1, 0), far, 0)

    @pl.when(qi >= 1)
    def _():
        k, v = kv(qi - 1)
        step(_dot_nt(q2, k) + bsub_ref[...], v)

    k, v = kv(qi)
    step(_dot_nt(q2, k) + bdiag_ref[...], v)

    on = acc_sc[...] / l_sc[...]
    o_ref[...] = on[:tile] - lam_ref[0] * on[tile:]


def _flash(q, kb, vb, kmeta, vmeta, bmeta, bsub, bdiag, lam, batch, seq):
    tile = ATT_TILE
    nq = seq // tile
    d_att = q.shape[1]
    head = lambda rows: pl.BlockSpec((rows, HEAD_DV), lambda b, h, qi: (0, h))
    per_head = lambda r, c: pl.BlockSpec((None, r, c), lambda b, h, qi: (h, 0, 0))
    return pl.pallas_call(
        functools.partial(_flash_kernel, tile=tile),
        grid=(batch, N_HEADS, nq),
        in_specs=[
            pl.BlockSpec(memory_space=pltpu.SMEM),
            pl.BlockSpec((tile, HEAD_DV), lambda b, h, qi: (b * nq + qi, h)),
            pl.BlockSpec((seq, HEAD_DV), lambda b, h, qi: (b, h)),
            pl.BlockSpec((seq, HEAD_DV), lambda b, h, qi: (b, h)),
            head(N_META), head(N_META),
            per_head(2 * tile, N_META), per_head(2 * tile, tile), per_head(2 * tile, tile),
        ],
        out_specs=pl.BlockSpec((tile, HEAD_DV), lambda b, h, qi: (b * nq + qi, h)),
        out_shape=jax.ShapeDtypeStruct((batch * seq, d_att), F32),
        scratch_shapes=[pltpu.VMEM((2 * tile, 1), F32), pltpu.VMEM((2 * tile, 1), F32),
                        pltpu.VMEM((2 * tile, HEAD_DV), F32)],
        compiler_params=pltpu.CompilerParams(
            dimension_semantics=("arbitrary", "arbitrary", "arbitrary")),
        name="prompt_attention",
    )(lam, q, kb, vb, kmeta, vmeta, bmeta, bsub, bdiag)


def _meta_attn_kernel(lam_ref, q_ref, k_ref, v_ref, bias_ref, o_ref):
    lam = lam_ref[0]
    lane = lax.broadcasted_iota(jnp.int32, (q_ref.shape[0], HEAD_DV), 1)
    for h in range(N_HEADS):
        c0 = h * HEAD_DV
        bias = bias_ref[h]
        q = q_ref[:, c0:c0 + HEAD_DV]
        k = k_ref[:, c0:c0 + HEAD_DV]
        zero = jnp.zeros_like(q)
        ps = []
        for qc in (jnp.where(lane < HEAD_DK, q, zero), jnp.where(lane >= HEAD_DK, q, zero)):
            s = _dot_nt(qc, k) + bias
            e = jnp.exp(s - jnp.max(s, axis=-1, keepdims=True))
            ps.append(e / jnp.sum(e, axis=-1, keepdims=True))
        wgt = (ps[0] - lam * ps[1]).astype(BF16)
        o_ref[:, c0:c0 + HEAD_DV] = _dot(wgt, v_ref[:, c0:c0 + HEAD_DV])


def _meta_attn(q, kb, vb, bias, lam):
    n, d_att = q.shape
    vm = pl.BlockSpec(memory_space=pltpu.VMEM)
    return pl.pallas_call(
        _meta_attn_kernel,
        in_specs=[pl.BlockSpec(memory_space=pltpu.SMEM), vm, vm, vm, vm],
        out_specs=vm,
        out_shape=jax.ShapeDtypeStruct((n, d_att), F32),
        name="meta_attention",
    )(lam, q, kb, vb, bias)


def _decode_kernel(pt_ref, lam_ref, q_ref, *rest, pages, n_steps):
    del pt_ref
    kp = rest[:pages]
    vp = rest[pages:2 * pages]
    kself_ref, vself_ref, blast_ref, bself_ref, o_ref, m_sc, l_sc, acc_sc = rest[2 * pages:]
    step_id = pl.program_id(1)
    rows = 2 * N_HEADS
    d_att = q_ref.shape[-1]
    row = lax.broadcasted_iota(jnp.int32, (rows, d_att), 0)
    lane = lax.broadcasted_iota(jnp.int32, (rows, d_att), 1)
    qb = jnp.broadcast_to(q_ref[...], (rows, d_att))
    seg = lax.shift_right_logical(lane, int(math.log2(HEAD_DK)))
    qbd = jnp.where(seg == row, qb, 0.0).astype(BF16)

    @pl.when(step_id == 0)
    def _():
        m_sc[...] = jnp.full(m_sc.shape, NEG, F32)
        l_sc[...] = jnp.zeros(l_sc.shape, F32)
        acc_sc[...] = jnp.zeros(acc_sc.shape, F32)

    def update(s_parts, v_parts):
        m_prev = m_sc[...]
        m_new = m_prev
        for s in s_parts:
            m_new = jnp.maximum(m_new, jnp.max(s, axis=-1, keepdims=True))
        alpha = jnp.exp(m_prev - m_new)
        l_new = alpha * l_sc[...]
        acc = alpha * acc_sc[...]
        for s, v in zip(s_parts, v_parts):
            p = jnp.exp(s - m_new)
            l_new = l_new + jnp.sum(p, axis=-1, keepdims=True)
            acc = acc + _dot(p.astype(BF16), v)
        m_sc[...] = m_new
        l_sc[...] = l_new
        acc_sc[...] = acc

    is_last = step_id == n_steps - 1
    page_size = blast_ref.shape[1]
    s_parts = []
    v_parts = []
    for g in range(pages):
        s = _dot(qbd, kp[g][...].astype(BF16))
        if g == pages - 1:
            s = s + jnp.where(is_last, blast_ref[...], 0.0)
        s_parts.append(s)
        v_heads = [vp[g][pl.ds(h, page_size, stride=N_HEADS), :] for h in range(N_HEADS)]
        v_parts.append(jnp.concatenate(v_heads, axis=1).astype(BF16))
    update(s_parts, v_parts)

    @pl.when(is_last)
    def _():
        update([_dot_nt(qbd, kself_ref[...]) + bself_ref[...]], [vself_ref[...]])
        on = acc_sc[...] / l_sc[...]
        own = lax.shift_right_logical(lane, int(math.log2(HEAD_DV))) == lax.shift_right_logical(row, 1)
        comp0 = (row & 1) == 0
        sel = jnp.where(own, jnp.where(comp0, 1.0, -lam_ref[0]), 0.0)
        o_ref[...] = jnp.sum(on * sel, axis=0, keepdims=True)


def _decode_attn(page_table, lam, q, cache_kt, cache_vr, layer, kself, vself, blast, bself):
    db, n_pages = page_table.shape
    d_att, page_size = cache_kt.shape[2], cache_kt.shape[3]
    pages = PAGES_PER_STEP
    n_steps = n_pages // pages
    rows = 2 * N_HEADS

    def page_spec(arr):
        return lambda g: pl.BlockSpec(
            (None, None) + arr.shape[2:], lambda b, s, pt: (layer, pt[b, s * pages + g], 0, 0))

    k_spec, v_spec = page_spec(cache_kt), page_spec(cache_vr)

    per_seq = lambda r: pl.BlockSpec((None, r, d_att), lambda b, s, pt: (b, 0, 0))
    whole = lambda shape: pl.BlockSpec(shape, lambda b, s, pt: (0,) * len(shape))
    grid_spec = pltpu.PrefetchScalarGridSpec(
        num_scalar_prefetch=1,
        grid=(db, n_steps),
        in_specs=([pl.BlockSpec(memory_space=pltpu.SMEM), per_seq(1)]
                  + [k_spec(g) for g in range(pages)]
                  + [v_spec(g) for g in range(pages)]
                  + [per_seq(rows), per_seq(rows),
                     whole((rows, page_size)), whole((rows, rows))]),
        out_specs=per_seq(1),
        scratch_shapes=[pltpu.VMEM((rows, 1), F32), pltpu.VMEM((rows, 1), F32),
                        pltpu.VMEM((rows, d_att), F32)],
    )
    return pl.pallas_call(
        functools.partial(_decode_kernel, pages=pages, n_steps=n_steps),
        grid_spec=grid_spec,
        out_shape=jax.ShapeDtypeStruct((db, 1, d_att), F32),
        compiler_params=pltpu.CompilerParams(
            dimension_semantics=("arbitrary", "arbitrary"), vmem_limit_bytes=VMEM_LIMIT),
        name="decode_attention",
    )(page_table, lam, q, *([cache_kt] * pages), *([cache_vr] * pages),
      kself, vself, blast, bself)


def _ln_swish(y, lnw, lnb):
    mu = jnp.mean(y, axis=-1, keepdims=True)
    yc = y - mu
    var = jnp.mean(yc * yc, axis=-1, keepdims=True)
    return _silu((yc * lax.rsqrt(var + 1e-5)) * lnw + lnb)


def _conv_kernel(u_ref, halo_ref, prefix_ref, w_ref, b_ref, lnw_ref, lnb_ref, z_ref, full_sc,
                 *, tm):
    ti = pl.program_id(1)
    full_sc[0:HALO, :] = jnp.where(ti == 0, prefix_ref[...], halo_ref[...])
    full_sc[HALO:HALO + tm, :] = u_ref[...]
    first = HALO - (CONV_K - 1)

    for r0 in range(0, tm, CONV_CHUNK):
        acc = full_sc[r0 + first:r0 + first + CONV_CHUNK, :] * w_ref[0:1, :]
        for k in range(1, CONV_K):
            lo = r0 + first + k
            acc = acc + full_sc[lo:lo + CONV_CHUNK, :] * w_ref[k:k + 1, :]
        z = _ln_swish(acc + b_ref[...], lnw_ref[...], lnb_ref[...])
        z_ref[r0:r0 + CONV_CHUNK, :] = z.astype(BF16)


def _conv(u, prefix, conv_w, conv_b, ln_w, ln_b, batch, seq):
    tm = ROW_TILE
    nt = seq // tm
    c = u.shape[1]
    ratio = tm // HALO
    vec = lambda: _resident((1, c))
    return pl.pallas_call(
        functools.partial(_conv_kernel, tm=tm),
        grid=(batch, nt),
        in_specs=[
            pl.BlockSpec((tm, c), lambda b, t: (b * nt + t, 0)),
            pl.BlockSpec((HALO, c), lambda b, t: (jnp.maximum((b * nt + t) * ratio - 1, 0), 0)),
            _resident((HALO, c)),
            _resident((CONV_K, c)), vec(), vec(), vec(),
        ],
        out_specs=pl.BlockSpec((tm, c), lambda b, t: (b * nt + t, 0)),
        out_shape=jax.ShapeDtypeStruct((batch * seq, c), BF16),
        scratch_shapes=[pltpu.VMEM((HALO + tm, c), F32)],
        compiler_params=pltpu.CompilerParams(dimension_semantics=("arbitrary", "arbitrary")),
        name="conv",
    )(u, u, prefix, conv_w, conv_b.reshape(1, c), ln_w.reshape(1, c), ln_b.reshape(1, c))


def _conv_small_kernel(full_ref, w_ref, b_ref, lnw_ref, lnb_ref, z_ref, *, nseq, t_out):
    for n in range(nseq):
        acc = full_ref[n, 0:t_out, :] * w_ref[0:1, :]
        for k in range(1, CONV_K):
            acc = acc + full_ref[n, k:k + t_out, :] * w_ref[k:k + 1, :]
        z_ref[n] = _ln_swish(acc + b_ref[...], lnw_ref[...], lnb_ref[...])


def _conv_small(full, conv_w, conv_b, ln_w, ln_b):
    nseq, t_full, c = full.shape
    t_out = t_full - (CONV_K - 1)
    vm = pl.BlockSpec(memory_space=pltpu.VMEM)
    return pl.pallas_call(
        functools.partial(_conv_small_kernel, nseq=nseq, t_out=t_out),
        in_specs=[vm] * 5,
        out_specs=vm,
        out_shape=jax.ShapeDtypeStruct((nseq, t_out, c), F32),
        name="conv_small",
    )(full, conv_w, conv_b.reshape(1, c), ln_w.reshape(1, c), ln_b.reshape(1, c))


def _out_kernel(z_ref, cg_ref, o_ref, ag_ref, h_ref, wpw_ref, sub_ref, wout_ref, y_ref,
                *, c_conv, post_scale):
    conv_y = _dot(z_ref[...], wpw_ref[...]) * _silu(cg_ref[...])
    sub = sub_ref[...]
    atts = []
    for h in range(N_HEADS):
        c0 = h * HEAD_DV
        oh = o_ref[:, c0:c0 + HEAD_DV]
        ms = jnp.mean(oh * oh, axis=-1, keepdims=True)
        att = (((oh * lax.rsqrt(ms + 1e-5)) * sub) * post_scale) * _silu(ag_ref[:, c0:c0 + HEAD_DV])
        atts.append(att.astype(BF16))
    mixed = jnp.concatenate([conv_y.astype(BF16)] + atts, axis=1)
    y_ref[...] = h_ref[...] + _dot(mixed, wout_ref[...])


def _out_proj(z, cg, o, ag, h, w_pw2_bf, subln_w, w_out_bf, lam_init, tm):
    rows, d = h.shape
    c_conv = z.shape[1]
    d_att = o.shape[1]
    blk = lambda w: pl.BlockSpec((tm, w), lambda i: (i, 0))
    return pl.pallas_call(
        functools.partial(_out_kernel, c_conv=c_conv, post_scale=1.0 - lam_init),
        grid=(rows // tm,),
        in_specs=[blk(c_conv), blk(c_conv), blk(d_att), blk(d_att), blk(d),
                  _resident((c_conv, c_conv)), _resident((1, HEAD_DV)),
                  _resident((c_conv + d_att, d))],
        out_specs=blk(d),
        out_shape=jax.ShapeDtypeStruct((rows, d), F32),
        compiler_params=pltpu.CompilerParams(
            dimension_semantics=("arbitrary",), vmem_limit_bytes=VMEM_LIMIT),
        name="out_proj",
    )(z, cg, o, ag, h, w_pw2_bf, subln_w.reshape(1, HEAD_DV), w_out_bf)


def _rmsnorm_kernel(x_ref, w_ref, y_ref):
    x = x_ref[...]
    ms = jnp.mean(x * x, axis=-1, keepdims=True)
    y_ref[...] = (x * lax.rsqrt(ms + 1e-6)) * w_ref[...]


def _rmsnorm(x, w, tm):
    rows, d = x.shape
    return pl.pallas_call(
        _rmsnorm_kernel,
        grid=(rows // tm,),
        in_specs=[pl.BlockSpec((tm, d), lambda i: (i, 0)), _resident((1, d))],
        out_specs=pl.BlockSpec((tm, d), lambda i: (i, 0)),
        out_shape=jax.ShapeDtypeStruct((rows, d), F32),
        name="final_norm",
    )(x, w.reshape(1, d))


def _bias_by_distance(rel_bias, n):
    d = jnp.arange(n)
    max_exact = N_BUCKETS // 2
    nf = jnp.maximum(d, 1).astype(F32)
    large = max_exact + (jnp.log(nf / max_exact) / math.log(MAX_DIST / max_exact)
                         * (N_BUCKETS - max_exact)).astype(jnp.int32)
    bucket = jnp.where(d < max_exact, d, jnp.minimum(large, N_BUCKETS - 1))
    t = rel_bias[bucket] - rel_bias[N_BUCKETS - 1]
    return t.T.astype(F32)


def _toeplitz(t, offset, rows, cols, causal):
    i = jnp.arange(rows)[:, None]
    j = jnp.arange(cols)[None, :]
    d = offset + i - j
    b = t[:, jnp.maximum(d, 0)]
    if causal:
        b = jnp.where((d >= 0)[None], b, NEG)
    return b


def kernel(x_prompt, x_sample, cache_k, cache_v, state_conv, page_table, meta_tokens, rel_bias,
           norm_w, w_in, conv_w, conv_b, conv_ln_w, conv_ln_b, w_pw2, lambda_q1, lambda_k1,
           lambda_q2, lambda_k2, subln_w, w_out, final_norm_w):
    batch, seq, d = x_prompt.shape
    depth = w_in.shape[0]
    db = x_sample.shape[0]
    c_conv = d // 2
    d_att = d - c_conv
    n_pool, page_size = cache_k.shape[1], cache_k.shape[2]
    n_pages = page_table.shape[1]
    plen = n_pages * page_size
    tile = ATT_TILE
    rows2 = 2 * N_HEADS

    cache_kt = jnp.transpose(cache_k, (0, 1, 3, 4, 5, 2)).reshape(depth, n_pool, d_att, page_size)
    cache_vr = cache_v.reshape(depth, n_pool, page_size * N_HEADS, HEAD_DV)

    t = _bias_by_distance(rel_bias, 2 * tile + N_META)
    twice = lambda b: jnp.concatenate([b, b], axis=1)
    bdiag = twice(_toeplitz(t, 0, tile, tile, True))
    bsub = twice(_toeplitz(t, tile, tile, tile, False))
    bmeta = twice(_toeplitz(t, N_META, tile, N_META, False))
    bias_meta = _toeplitz(t, 0, N_META, N_META, True)
    t_rows = jnp.repeat(t, 2, axis=0)
    blast = t_rows[:, page_size - jnp.arange(page_size)]
    bself = jnp.where(jnp.arange(rows2)[None, :] == 0, t_rows[:, 0:1], NEG)

    hp = x_prompt.reshape(batch * seq, d)
    side = jnp.concatenate([meta_tokens.astype(F32), x_sample.reshape(db, d),
                            jnp.zeros((SIDE_ROWS - N_META - db, d), F32)], axis=0)
    kp_l, vp_l, cp_l, ks_l, vs_l, cs_l = [], [], [], [], [], []
    for l in range(depth):
        lam_init = 0.8 - 0.6 * math.exp(-0.3 * l)
        lam = (jnp.exp(jnp.sum(lambda_q1[l].astype(F32) * lambda_k1[l].astype(F32)))
               - jnp.exp(jnp.sum(lambda_q2[l].astype(F32) * lambda_k2[l].astype(F32)))
               + lam_init).reshape(1).astype(F32)
        w_in_bf = w_in[l].astype(BF16)
        w_pw2_bf = w_pw2[l].astype(BF16)
        w_out_bf = w_out[l].astype(BF16)

        us, cgs, qs, ks, vs, ags, kbs, vbs = _inproj(side, norm_w[l], w_in_bf, SIDE_ROWS)
        u_meta = us[:N_META]
        full_meta = jnp.concatenate([jnp.zeros((CONV_K - 1, c_conv), F32), u_meta], axis=0)[None]
        z_meta = _conv_small(full_meta, conv_w[l], conv_b[l], conv_ln_w[l], conv_ln_b[l])[0]
        full_dec = jnp.concatenate([state_conv[l], us[N_META:N_META + db][:, None, :]], axis=1)
        z_dec = _conv_small(full_dec, conv_w[l], conv_b[l], conv_ln_w[l], conv_ln_b[l])[:, 0]
        o_meta = _meta_attn(qs[:N_META], kbs[:N_META], vbs[:N_META], bias_meta, lam)
        pad_self = jnp.zeros((db, rows2 - 1, d_att), BF16)
        kself = jnp.concatenate([kbs[N_META:N_META + db][:, None, :], pad_self], axis=1)
        vself = jnp.concatenate([vbs[N_META:N_META + db][:, None, :], pad_self], axis=1)
        q_dec = qs[N_META:N_META + db][:, None, :].astype(F32)
        o_dec = _decode_attn(page_table, lam, q_dec, cache_kt, cache_vr, l, kself, vself,
                             blast, bself)[:, 0]
        n_pad = SIDE_ROWS - N_META - db
        z_side = jnp.concatenate([z_meta, z_dec, jnp.zeros((n_pad, c_conv), F32)],
                                 axis=0).astype(BF16)
        o_side = jnp.concatenate([o_meta, o_dec, jnp.zeros((n_pad, d_att), F32)], axis=0)

        u, cg, q, k, v, ag, kb, vb = _inproj(hp, norm_w[l], w_in_bf, ROW_TILE)
        prefix = jnp.concatenate([jnp.zeros((HALO - N_META, c_conv), F32), u_meta], axis=0)
        z = _conv(u, prefix, conv_w[l], conv_b[l], conv_ln_w[l], conv_ln_b[l], batch, seq)
        o = _flash(q, kb, vb, kbs[:N_META], vbs[:N_META], bmeta, bsub, bdiag, lam, batch, seq)
        hp = _out_proj(z, cg, o, ag, hp, w_pw2_bf, subln_w[l], w_out_bf, lam_init, ROW_TILE)
        side = _out_proj(z_side, cgs, o_side, ags, side, w_pw2_bf, subln_w[l], w_out_bf,
                         lam_init, SIDE_ROWS)

        k_meta = jnp.broadcast_to(ks[:N_META][None], (batch, N_META, d_att))
        v_meta = jnp.broadcast_to(vs[:N_META][None], (batch, N_META, d_att))
        kp_l.append(jnp.concatenate([k_meta, k.reshape(batch, seq, d_att)], axis=1))
        vp_l.append(jnp.concatenate([v_meta, v.reshape(batch, seq, d_att)], axis=1))
        cp_l.append(u.reshape(batch, seq, c_conv)[:, seq - (CONV_K - 1):])
        ks_l.append(ks[N_META:N_META + db])
        vs_l.append(vs[N_META:N_META + db])
        cs_l.append(full_dec[:, 1:])

    y_prompt = _rmsnorm(hp, final_norm_w, ROW_TILE).reshape(batch, seq, d)
    y_side = _rmsnorm(side, final_norm_w, SIDE_ROWS)
    y_sample = y_side[N_META:N_META + db].reshape(db, 1, d)
    lp = seq + N_META
    return (y_prompt, y_sample,
            jnp.stack(kp_l).reshape(depth, batch, lp, N_HEADS, 2, HEAD_DK),
            jnp.stack(vp_l).reshape(depth, batch, lp, N_HEADS, HEAD_DV),
            jnp.stack(cp_l),
            jnp.stack(ks_l).reshape(depth, db, 1, N_HEADS, 2, HEAD_DK),
            jnp.stack(vs_l).reshape(depth, db, 1, N_HEADS, HEAD_DV),
            jnp.stack(cs_l))
```

```python
import functools
import math

import jax
import jax.numpy as jnp
from jax import lax
from jax.experimental import pallas as pl
from jax.experimental.pallas import tpu as pltpu

F32 = jnp.float32
BF16 = jnp.bfloat16

N_META = 16
N_HEADS = 8
HEAD_DV = 128
HEAD_DK = 64
CONV_K = 31
N_BUCKETS = 32
MAX_DIST = 128
NEG = -1e30
SUBLANES = 8
SIDE_ROWS = 32
HALO = 32
ROW_TILE = 256
ATT_TILE = 512
META_PAD = 128
CONV_CHUNK = 32
PAGES_PER_STEP = 8
VMEM_LIMIT = 56 * 1024 * 1024


def _dot(a, b):
    return jnp.dot(a, b, preferred_element_type=F32)


def _dot_nt(a, b):
    return lax.dot_general(a, b, (((1,), (1,)), ((), ())), preferred_element_type=F32)


def _silu(x):
    return x * jax.nn.sigmoid(x)


def _resident(shape):
    return pl.BlockSpec(shape, lambda *_: (0,) * len(shape), pipeline_mode=pl.Buffered(1))


def _resident_layer(stacked, layer):
    shape = stacked.shape[1:]
    return pl.BlockSpec((None,) + shape, lambda *_: (layer,) + (0,) * len(shape),
                        pipeline_mode=pl.Buffered(1))


def _inproj_kernel(x_ref, nw_ref, w_ref, u_ref, cg_ref, q_ref, k_ref, v_ref, ag_ref,
                   kb_ref, vb_ref, *, c_conv, d_att, scale):
    x = x_ref[...]
    ms = jnp.mean(x * x, axis=-1, keepdims=True)
    xb = ((x * lax.rsqrt(ms + 1e-6)) * nw_ref[...]).astype(BF16)

    def mm(c0, width):
        return _dot(xb, w_ref[:, c0:c0 + width])

    a = mm(0, c_conv)
    g = mm(c_conv, c_conv)
    u_ref[...] = a * jax.nn.sigmoid(g)
    cg_ref[...] = mm(2 * c_conv, c_conv)
    base = 3 * c_conv
    q_ref[...] = (mm(base, d_att) * scale).astype(BF16)
    k = mm(base + d_att, d_att)
    k_ref[...] = k
    kb_ref[...] = k.astype(BF16)
    v = mm(base + 2 * d_att, d_att)
    v_ref[...] = v
    vb_ref[...] = v.astype(BF16)
    ag_ref[...] = mm(base + 3 * d_att, d_att)


def _inproj(x, norm_w, w_in_bf, layer, tm):
    rows, d = x.shape
    c_conv = d // 2
    d_att = d - c_conv
    row_f32 = lambda w: pl.BlockSpec((tm, w), lambda i: (i, 0))
    shapes = [
        jax.ShapeDtypeStruct((rows, c_conv), F32),
        jax.ShapeDtypeStruct((rows, c_conv), F32),
        jax.ShapeDtypeStruct((rows, d_att), BF16),
        jax.ShapeDtypeStruct((rows, d_att), F32),
        jax.ShapeDtypeStruct((rows, d_att), F32),
        jax.ShapeDtypeStruct((rows, d_att), F32),
        jax.ShapeDtypeStruct((rows, d_att), BF16),
        jax.ShapeDtypeStruct((rows, d_att), BF16),
    ]
    return pl.pallas_call(
        functools.partial(_inproj_kernel, c_conv=c_conv, d_att=d_att, scale=HEAD_DK ** -0.5),
        grid=(rows // tm,),
        in_specs=[row_f32(d), _resident((1, d)), _resident_layer(w_in_bf, layer)],
        out_specs=[row_f32(s.shape[1]) for s in shapes],
        out_shape=shapes,
        compiler_params=pltpu.CompilerParams(
            dimension_semantics=("arbitrary",), vmem_limit_bytes=VMEM_LIMIT),
        name="inproj",
    )(x, norm_w.reshape(1, d), w_in_bf)


def _flash_kernel(lam_ref, q_ref, k_ref, v_ref, km_ref, vm_ref, bmeta_ref, bsub_ref, bdiag_ref,
                  o_ref, st_sc, vxt_sc, *, tile, n_q):
    qi = pl.program_id(2)
    qt = q_ref[...].T
    row = lax.broadcasted_iota(jnp.int32, qt.shape, 0)
    zero = jnp.zeros_like(qt)
    q2t = jnp.concatenate([jnp.where(row < HEAD_DK, qt, zero),
                           jnp.where(row >= HEAD_DK, qt, zero)], axis=1)

    pad = jnp.zeros((META_PAD - N_META, HEAD_DV), BF16)
    k_meta = jnp.concatenate([km_ref[...], pad], axis=0)
    both = lambda b: jnp.concatenate([b, b], axis=1)

    @pl.when(qi == 0)
    def _():
        v_meta = jnp.concatenate([vm_ref[...], pad], axis=0)
        vxt_sc[0:HEAD_DV, 0:META_PAD] = v_meta.T
        vxt_sc[0:HEAD_DV, META_PAD:] = v_ref[...].T
        vxt_sc[HEAD_DV:, :] = jnp.ones((HEAD_DV, vxt_sc.shape[1]), BF16)

    def variant(c):
        s = _dot(k_meta, q2t) + both(bmeta_ref[0 if c == 0 else 1])
        st_sc[0:META_PAD, :] = s
        m = jnp.max(s, axis=0, keepdims=True)
        for ki in range(c + 1):
            s = _dot(k_ref[ki * tile:(ki + 1) * tile, :], q2t)
            if ki == c:
                s = s + both(bdiag_ref[...])
            elif ki == c - 1:
                s = s + both(bsub_ref[...])
            lo = META_PAD + ki * tile
            st_sc[lo:lo + tile, :] = s
            m = jnp.maximum(m, jnp.max(s, axis=0, keepdims=True))

        p = jnp.exp(st_sc[0:META_PAD, :] - m).astype(BF16)
        acc = _dot(vxt_sc[:, 0:META_PAD], p)
        for ki in range(c + 1):
            lo = META_PAD + ki * tile
            p = jnp.exp(st_sc[lo:lo + tile, :] - m).astype(BF16)
            acc = acc + _dot(vxt_sc[:, lo:lo + tile], p)
        on = acc[0:HEAD_DV] / acc[HEAD_DV:]
        o_ref[...] = (on[:, :tile] - lam_ref[0] * on[:, tile:]).T

    for c in range(n_q):
        pl.when(qi == c)(functools.partial(variant, c))


def _flash(q, kb, vb, kmeta, vmeta, bmeta, bsub, bdiag, lam, batch, seq):
    tile = ATT_TILE
    nq = seq // tile
    d_att = q.shape[1]
    head = lambda rows: pl.BlockSpec((rows, HEAD_DV), lambda b, h, qi: (0, h))
    per_head = lambda r, c: pl.BlockSpec((None, r, c), lambda b, h, qi: (h, 0, 0))
    return pl.pallas_call(
        functools.partial(_flash_kernel, tile=tile, n_q=nq),
        grid=(batch, N_HEADS, nq),
        in_specs=[
            pl.BlockSpec(memory_space=pltpu.SMEM),
            pl.BlockSpec((tile, HEAD_DV), lambda b, h, qi: (b * nq + qi, h)),
            pl.BlockSpec((seq, HEAD_DV), lambda b, h, qi: (b, h)),
            pl.BlockSpec((seq, HEAD_DV), lambda b, h, qi: (b, h)),
            head(N_META), head(N_META),
            pl.BlockSpec((None, 2, META_PAD, tile), lambda b, h, qi: (h, 0, 0, 0)),
            per_head(tile, tile), per_head(tile, tile),
        ],
        out_specs=pl.BlockSpec((tile, HEAD_DV), lambda b, h, qi: (b * nq + qi, h)),
        out_shape=jax.ShapeDtypeStruct((batch * seq, d_att), F32),
        scratch_shapes=[pltpu.VMEM((META_PAD + seq, 2 * tile), F32),
                        pltpu.VMEM((2 * HEAD_DV, META_PAD + seq), BF16)],
        compiler_params=pltpu.CompilerParams(
            dimension_semantics=("arbitrary", "arbitrary", "arbitrary"),
            vmem_limit_bytes=VMEM_LIMIT),
        name="prompt_attention",
    )(lam, q, kb, vb, kmeta, vmeta, bmeta, bsub, bdiag)


def _meta_attn_kernel(lam_ref, q_ref, k_ref, v_ref, bias_ref, o_ref):
    lam = lam_ref[0]
    lane = lax.broadcasted_iota(jnp.int32, (q_ref.shape[0], HEAD_DV), 1)
    for h in range(N_HEADS):
        c0 = h * HEAD_DV
        bias = bias_ref[h]
        q = q_ref[:, c0:c0 + HEAD_DV]
        k = k_ref[:, c0:c0 + HEAD_DV]
        zero = jnp.zeros_like(q)
        ps = []
        for qc in (jnp.where(lane < HEAD_DK, q, zero), jnp.where(lane >= HEAD_DK, q, zero)):
            s = _dot_nt(qc, k) + bias
            e = jnp.exp(s - jnp.max(s, axis=-1, keepdims=True))
            ps.append(e / jnp.sum(e, axis=-1, keepdims=True))
        wgt = (ps[0] - lam * ps[1]).astype(BF16)
        o_ref[:, c0:c0 + HEAD_DV] = _dot(wgt, v_ref[:, c0:c0 + HEAD_DV])


def _meta_attn(q, kb, vb, bias, lam):
    n, d_att = q.shape
    vm = pl.BlockSpec(memory_space=pltpu.VMEM)
    return pl.pallas_call(
        _meta_attn_kernel,
        in_specs=[pl.BlockSpec(memory_space=pltpu.SMEM), vm, vm, vm, vm],
        out_specs=vm,
        out_shape=jax.ShapeDtypeStruct((n, d_att), F32),
        name="meta_attention",
    )(lam, q, kb, vb, bias)


def _decode_kernel(pt_ref, lam_ref, q_ref, *rest, pages, n_steps):
    del pt_ref
    kp = rest[:pages]
    vp = rest[pages:2 * pages]
    kself_ref, vself_ref, blast_ref, bself_ref, o_ref, m_sc, l_sc, acc_sc = rest[2 * pages:]
    step_id = pl.program_id(1)
    rows = 2 * N_HEADS
    d_att = q_ref.shape[-1]
    row = lax.broadcasted_iota(jnp.int32, (rows, d_att), 0)
    lane = lax.broadcasted_iota(jnp.int32, (rows, d_att), 1)
    qb = jnp.broadcast_to(q_ref[...], (rows, d_att))
    seg = lax.shift_right_logical(lane, int(math.log2(HEAD_DK)))
    qbd = jnp.where(seg == row, qb, 0.0).astype(BF16)

    @pl.when(step_id == 0)
    def _():
        m_sc[...] = jnp.full(m_sc.shape, NEG, F32)
        l_sc[...] = jnp.zeros(l_sc.shape, F32)
        acc_sc[...] = jnp.zeros(acc_sc.shape, F32)

    def update(s_parts, v_parts):
        m_prev = m_sc[...]
        m_new = m_prev
        for s in s_parts:
            m_new = jnp.maximum(m_new, jnp.max(s, axis=-1, keepdims=True))
        alpha = jnp.exp(m_prev - m_new)
        l_new = alpha * l_sc[...]
        acc = alpha * acc_sc[...]
        for s, v in zip(s_parts, v_parts):
            p = jnp.exp(s - m_new)
            l_new = l_new + jnp.sum(p, axis=-1, keepdims=True)
            acc = acc + _dot(p.astype(BF16), v)
        m_sc[...] = m_new
        l_sc[...] = l_new
        acc_sc[...] = acc

    is_last = step_id == n_steps - 1
    page_size = blast_ref.shape[1]
    s_parts = []
    v_parts = []
    for g in range(pages):
        s = _dot(qbd, kp[g][...].astype(BF16))
        if g == pages - 1:
            s = s + jnp.where(is_last, blast_ref[...], 0.0)
        s_parts.append(s)
        v_heads = [vp[g][pl.ds(h, page_size, stride=N_HEADS), :] for h in range(N_HEADS)]
        v_parts.append(jnp.concatenate(v_heads, axis=1).astype(BF16))
    update(s_parts, v_parts)

    @pl.when(is_last)
    def _():
        update([_dot_nt(qbd, kself_ref[...]) + bself_ref[...]], [vself_ref[...]])
        on = acc_sc[...] / l_sc[...]
        own = lax.shift_right_logical(lane, int(math.log2(HEAD_DV))) == lax.shift_right_logical(row, 1)
        comp0 = (row & 1) == 0
        sel = jnp.where(own, jnp.where(comp0, 1.0, -lam_ref[0]), 0.0)
        o_ref[...] = jnp.sum(on * sel, axis=0, keepdims=True)


def _decode_attn(page_table, lam, q, cache_kt, cache_vr, layer, kself, vself, blast, bself):
    db, n_pages = page_table.shape
    d_att, page_size = cache_kt.shape[2], cache_kt.shape[3]
    pages = PAGES_PER_STEP
    n_steps = n_pages // pages
    rows = 2 * N_HEADS

    def page_spec(arr):
        return lambda g: pl.BlockSpec(
            (None, None) + arr.shape[2:], lambda b, s, pt: (layer, pt[b, s * pages + g], 0, 0))

    k_spec, v_spec = page_spec(cache_kt), page_spec(cache_vr)

    per_seq = lambda r: pl.BlockSpec((None, r, d_att), lambda b, s, pt: (b, 0, 0))
    whole = lambda shape: pl.BlockSpec(shape, lambda b, s, pt: (0,) * len(shape))
    grid_spec = pltpu.PrefetchScalarGridSpec(
        num_scalar_prefetch=1,
        grid=(db, n_steps),
        in_specs=([pl.BlockSpec(memory_space=pltpu.SMEM), per_seq(1)]
                  + [k_spec(g) for g in range(pages)]
                  + [v_spec(g) for g in range(pages)]
                  + [per_seq(rows), per_seq(rows),
                     whole((rows, page_size)), whole((rows, rows))]),
        out_specs=per_seq(1),
        scratch_shapes=[pltpu.VMEM((rows, 1), F32), pltpu.VMEM((rows, 1), F32),
                        pltpu.VMEM((rows, d_att), F32)],
    )
    return pl.pallas_call(
        functools.partial(_decode_kernel, pages=pages, n_steps=n_steps),
        grid_spec=grid_spec,
        out_shape=jax.ShapeDtypeStruct((db, 1, d_att), F32),
        compiler_params=pltpu.CompilerParams(
            dimension_semantics=("arbitrary", "arbitrary"), vmem_limit_bytes=VMEM_LIMIT),
        name="decode_attention",
    )(page_table, lam, q, *([cache_kt] * pages), *([cache_vr] * pages),
      kself, vself, blast, bself)


def _ln_swish(y, lnw, lnb):
    mu = jnp.mean(y, axis=-1, keepdims=True)
    yc = y - mu
    var = jnp.mean(yc * yc, axis=-1, keepdims=True)
    return _silu((yc * lax.rsqrt(var + 1e-5)) * lnw + lnb)


def _conv_kernel(u_ref, halo_ref, prefix_ref, w_ref, b_ref, lnw_ref, lnb_ref, z_ref, full_sc,
                 shift_sc, *, tm):
    ti = pl.program_id(1)
    full_sc[0:HALO, :] = jnp.where(ti == 0, prefix_ref[...], halo_ref[...])
    full_sc[HALO:HALO + tm, :] = u_ref[...]
    span = shift_sc.shape[1]
    for r in range(1, SUBLANES):
        shift_sc[r - 1] = full_sc[r:r + span, :]
    first = HALO - (CONV_K - 1)

    for r0 in range(0, tm, CONV_CHUNK):
        acc = None
        for k in range(CONV_K):
            r = (first + k) % SUBLANES
            base = r0 + first + k - r
            if r == 0:
                window = full_sc[base:base + CONV_CHUNK, :]
            else:
                window = shift_sc[r - 1, base:base + CONV_CHUNK, :]
            term = window * w_ref[k:k + 1, :]
            acc = term if acc is None else acc + term
        z = _ln_swish(acc + b_ref[...], lnw_ref[...], lnb_ref[...])
        z_ref[r0:r0 + CONV_CHUNK, :] = z.astype(BF16)


def _conv(u, prefix, conv_w, conv_b, ln_w, ln_b, batch, seq):
    tm = ROW_TILE
    nt = seq // tm
    c = u.shape[1]
    ratio = tm // HALO
    vec = lambda: _resident((1, c))
    return pl.pallas_call(
        functools.partial(_conv_kernel, tm=tm),
        grid=(batch, nt),
        in_specs=[
            pl.BlockSpec((tm, c), lambda b, t: (b * nt + t, 0)),
            pl.BlockSpec((HALO, c), lambda b, t: (jnp.maximum((b * nt + t) * ratio - 1, 0), 0)),
            _resident((HALO, c)),
            _resident((CONV_K, c)), vec(), vec(), vec(),
        ],
        out_specs=pl.BlockSpec((tm, c), lambda b, t: (b * nt + t, 0)),
        out_shape=jax.ShapeDtypeStruct((batch * seq, c), BF16),
        scratch_shapes=[pltpu.VMEM((HALO + tm, c), F32),
                        pltpu.VMEM((SUBLANES - 1, HALO + tm - SUBLANES, c), F32)],
        compiler_params=pltpu.CompilerParams(
            dimension_semantics=("arbitrary", "arbitrary"), vmem_limit_bytes=VMEM_LIMIT),
        name="conv",
    )(u, u, prefix, conv_w, conv_b.reshape(1, c), ln_w.reshape(1, c), ln_b.reshape(1, c))


def _conv_small_kernel(full_ref, w_ref, b_ref, lnw_ref, lnb_ref, z_ref, *, nseq, t_out):
    for n in range(nseq):
        acc = full_ref[n, 0:t_out, :] * w_ref[0:1, :]
        for k in range(1, CONV_K):
            acc = acc + full_ref[n, k:k + t_out, :] * w_ref[k:k + 1, :]
        z_ref[n] = _ln_swish(acc + b_ref[...], lnw_ref[...], lnb_ref[...])


def _conv_small(full, conv_w, conv_b, ln_w, ln_b):
    nseq, t_full, c = full.shape
    t_out = t_full - (CONV_K - 1)
    vm = pl.BlockSpec(memory_space=pltpu.VMEM)
    return pl.pallas_call(
        functools.partial(_conv_small_kernel, nseq=nseq, t_out=t_out),
        in_specs=[vm] * 5,
        out_specs=vm,
        out_shape=jax.ShapeDtypeStruct((nseq, t_out, c), F32),
        name="conv_small",
    )(full, conv_w, conv_b.reshape(1, c), ln_w.reshape(1, c), ln_b.reshape(1, c))


def _out_kernel(z_ref, cg_ref, o_ref, ag_ref, h_ref, wpw_ref, sub_ref, wout_ref, y_ref,
                *, c_conv, post_scale):
    conv_y = _dot(z_ref[...], wpw_ref[...]) * _silu(cg_ref[...])
    sub = sub_ref[...]
    atts = []
    for h in range(N_HEADS):
        c0 = h * HEAD_DV
        oh = o_ref[:, c0:c0 + HEAD_DV]
        ms = jnp.mean(oh * oh, axis=-1, keepdims=True)
        att = (((oh * lax.rsqrt(ms + 1e-5)) * sub) * post_scale) * _silu(ag_ref[:, c0:c0 + HEAD_DV])
        atts.append(att.astype(BF16))
    mixed = jnp.concatenate([conv_y.astype(BF16)] + atts, axis=1)
    y_ref[...] = h_ref[...] + _dot(mixed, wout_ref[...])


def _out_proj(z, cg, o, ag, h, w_pw2_bf, subln_w, w_out_bf, layer, lam_init, tm):
    rows, d = h.shape
    c_conv = z.shape[1]
    d_att = o.shape[1]
    blk = lambda w: pl.BlockSpec((tm, w), lambda i: (i, 0))
    return pl.pallas_call(
        functools.partial(_out_kernel, c_conv=c_conv, post_scale=1.0 - lam_init),
        grid=(rows // tm,),
        in_specs=[blk(c_conv), blk(c_conv), blk(d_att), blk(d_att), blk(d),
                  _resident_layer(w_pw2_bf, layer), _resident((1, HEAD_DV)),
                  _resident_layer(w_out_bf, layer)],
        out_specs=blk(d),
        out_shape=jax.ShapeDtypeStruct((rows, d), F32),
        compiler_params=pltpu.CompilerParams(
            dimension_semantics=("arbitrary",), vmem_limit_bytes=VMEM_LIMIT),
        name="out_proj",
    )(z, cg, o, ag, h, w_pw2_bf, subln_w.reshape(1, HEAD_DV), w_out_bf)


def _rmsnorm_kernel(x_ref, w_ref, y_ref):
    x = x_ref[...]
    ms = jnp.mean(x * x, axis=-1, keepdims=True)
    y_ref[...] = (x * lax.rsqrt(ms + 1e-6)) * w_ref[...]


def _rmsnorm(x, w, tm):
    rows, d = x.shape
    return pl.pallas_call(
        _rmsnorm_kernel,
        grid=(rows // tm,),
        in_specs=[pl.BlockSpec((tm, d), lambda i: (i, 0)), _resident((1, d))],
        out_specs=pl.BlockSpec((tm, d), lambda i: (i, 0)),
        out_shape=jax.ShapeDtypeStruct((rows, d), F32),
        name="final_norm",
    )(x, w.reshape(1, d))


def _bias_by_distance(rel_bias, n):
    d = jnp.arange(n)
    max_exact = N_BUCKETS // 2
    nf = jnp.maximum(d, 1).astype(F32)
    large = max_exact + (jnp.log(nf / max_exact) / math.log(MAX_DIST / max_exact)
                         * (N_BUCKETS - max_exact)).astype(jnp.int32)
    bucket = jnp.where(d < max_exact, d, jnp.minimum(large, N_BUCKETS - 1))
    t = rel_bias[bucket] - rel_bias[N_BUCKETS - 1]
    return t.T.astype(F32)


def _toeplitz(t, offset, rows, cols):
    heads = t.shape[0]
    n_diag = rows + cols - 1
    lo = offset - (cols - 1)
    n_neg = max(0, -lo)
    diag = jnp.concatenate([jnp.full((heads, n_neg), NEG, F32),
                            t[:, lo + n_neg:lo + n_diag]], axis=1)
    period = jnp.concatenate([diag[:, ::-1], jnp.zeros((heads, 1), F32)], axis=1)
    skew = jnp.tile(period, (1, rows))[:, :rows * n_diag].reshape(heads, rows, n_diag)
    return skew[:, :, rows - 1:rows - 1 + cols]


def _meta_bias(t, tile):
    near = jnp.swapaxes(_toeplitz(t, N_META, tile, N_META), 1, 2)
    far = jnp.zeros_like(near)
    masked = jnp.full((near.shape[0], META_PAD - N_META, tile), NEG, F32)
    return jnp.stack([jnp.concatenate([near, masked], axis=1),
                      jnp.concatenate([far, masked], axis=1)], axis=1)


def kernel(x_prompt, x_sample, cache_k, cache_v, state_conv, page_table, meta_tokens, rel_bias,
           norm_w, w_in, conv_w, conv_b, conv_ln_w, conv_ln_b, w_pw2, lambda_q1, lambda_k1,
           lambda_q2, lambda_k2, subln_w, w_out, final_norm_w):
    batch, seq, d = x_prompt.shape
    depth = w_in.shape[0]
    db = x_sample.shape[0]
    c_conv = d // 2
    d_att = d - c_conv
    n_pool, page_size = cache_k.shape[1], cache_k.shape[2]
    n_pages = page_table.shape[1]
    plen = n_pages * page_size
    tile = ATT_TILE
    rows2 = 2 * N_HEADS

    cache_kt = jnp.transpose(cache_k, (0, 1, 3, 4, 5, 2)).reshape(depth, n_pool, d_att, page_size)
    cache_vr = cache_v.reshape(depth, n_pool, page_size * N_HEADS, HEAD_DV)

    t = _bias_by_distance(rel_bias, 2 * tile + N_META)
    bdiag = jnp.swapaxes(_toeplitz(t, 0, tile, tile), 1, 2)
    bsub = jnp.swapaxes(_toeplitz(t, tile, tile, tile), 1, 2)
    bmeta = _meta_bias(t, tile)
    bias_meta = _toeplitz(t, 0, N_META, N_META)
    t_rows = jnp.repeat(t, 2, axis=0)
    blast = t_rows[:, page_size:0:-1]
    bself = jnp.where(jnp.arange(rows2)[None, :] == 0, t_rows[:, 0:1], NEG)

    hp = x_prompt.reshape(batch * seq, d)
    side = jnp.concatenate([meta_tokens.astype(F32), x_sample.reshape(db, d),
                            jnp.zeros((SIDE_ROWS - N_META - db, d), F32)], axis=0)
    w_in_bf = w_in.astype(BF16)
    w_pw2_bf = w_pw2.astype(BF16)
    w_out_bf = w_out.astype(BF16)
    kp_l, vp_l, cp_l, ks_l, vs_l, cs_l = [], [], [], [], [], []
    for l in range(depth):
        lam_init = 0.8 - 0.6 * math.exp(-0.3 * l)
        lam = (jnp.exp(jnp.sum(lambda_q1[l].astype(F32) * lambda_k1[l].astype(F32)))
               - jnp.exp(jnp.sum(lambda_q2[l].astype(F32) * lambda_k2[l].astype(F32)))
               + lam_init).reshape(1).astype(F32)

        us, cgs, qs, ks, vs, ags, kbs, vbs = _inproj(side, norm_w[l], w_in_bf, l, SIDE_ROWS)
        u_meta = us[:N_META]
        full_meta = jnp.concatenate([jnp.zeros((CONV_K - 1, c_conv), F32), u_meta], axis=0)[None]
        z_meta = _conv_small(full_meta, conv_w[l], conv_b[l], conv_ln_w[l], conv_ln_b[l])[0]
        full_dec = jnp.concatenate([state_conv[l], us[N_META:N_META + db][:, None, :]], axis=1)
        z_dec = _conv_small(full_dec, conv_w[l], conv_b[l], conv_ln_w[l], conv_ln_b[l])[:, 0]
        o_meta = _meta_attn(qs[:N_META], kbs[:N_META], vbs[:N_META], bias_meta, lam)
        pad_self = jnp.zeros((db, rows2 - 1, d_att), BF16)
        kself = jnp.concatenate([kbs[N_META:N_META + db][:, None, :], pad_self], axis=1)
        vself = jnp.concatenate([vbs[N_META:N_META + db][:, None, :], pad_self], axis=1)
        q_dec = qs[N_META:N_META + db][:, None, :].astype(F32)
        o_dec = _decode_attn(page_table, lam, q_dec, cache_kt, cache_vr, l, kself, vself,
                             blast, bself)[:, 0]
        n_pad = SIDE_ROWS - N_META - db
        z_side = jnp.concatenate([z_meta, z_dec, jnp.zeros((n_pad, c_conv), F32)],
                                 axis=0).astype(BF16)
        o_side = jnp.concatenate([o_meta, o_dec, jnp.zeros((n_pad, d_att), F32)], axis=0)

        u, cg, q, k, v, ag, kb, vb = _inproj(hp, norm_w[l], w_in_bf, l, ROW_TILE)
        prefix = jnp.concatenate([jnp.zeros((HALO - N_META, c_conv), F32), u_meta], axis=0)
        z = _conv(u, prefix, conv_w[l], conv_b[l], conv_ln_w[l], conv_ln_b[l], batch, seq)
        o = _flash(q, kb, vb, kbs[:N_META], vbs[:N_META], bmeta, bsub, bdiag, lam, batch, seq)
        hp = _out_proj(z, cg, o, ag, hp, w_pw2_bf, subln_w[l], w_out_bf, l, lam_init, ROW_TILE)
        side = _out_proj(z_side, cgs, o_side, ags, side, w_pw2_bf, subln_w[l], w_out_bf, l,
                         lam_init, SIDE_ROWS)

        k_meta = jnp.broadcast_to(ks[:N_META][None], (batch, N_META, d_att))
        v_meta = jnp.broadcast_to(vs[:N_META][None], (batch, N_META, d_att))
        kp_l.append(jnp.concatenate([k_meta, k.reshape(batch, seq, d_att)], axis=1))
        vp_l.append(jnp.concatenate([v_meta, v.reshape(batch, seq, d_att)], axis=1))
        cp_l.append(u.reshape(batch, seq, c_conv)[:, seq - (CONV_K - 1):])
        ks_l.append(ks[N_META:N_META + db])
        vs_l.append(vs[N_META:N_META + db])
        cs_l.append(full_dec[:, 1:])

    y_prompt = _rmsnorm(hp, final_norm_w, ROW_TILE).reshape(batch, seq, d)
    y_side = _rmsnorm(side, final_norm_w, SIDE_ROWS)
    y_sample = y_side[N_META:N_META + db].reshape(db, 1, d)
    lp = seq + N_META
    return (y_prompt, y_sample,
            jnp.stack(kp_l).reshape(depth, batch, lp, N_HEADS, 2, HEAD_DK),
            jnp.stack(vp_l).reshape(depth, batch, lp, N_HEADS, HEAD_DV),
            jnp.stack(cp_l),
            jnp.stack(ks_l).reshape(depth, db, 1, N_HEADS, 2, HEAD_DK),
            jnp.stack(vs_l).reshape(depth, db, 1, N_HEADS, HEAD_DV),
            jnp.stack(cs_l))
```

```python
import functools
import math

import jax
import jax.numpy as jnp
from jax import lax
from jax.experimental import pallas as pl
from jax.experimental.pallas import tpu as pltpu

F32 = jnp.float32
BF16 = jnp.bfloat16

N_META = 16
N_HEADS = 8
HEAD_DV = 128
HEAD_DK = 64
CONV_K = 31
N_BUCKETS = 32
MAX_DIST = 128
NEG = -1e30
SUBLANES = 8
BF16_ROWS = 16
SIDE_ROWS = 32
HALO = 32
ROW_TILE = 256
ATT_TILE = 512
ATT_HEADS_PER_STEP = 2
META_PAD = 128
CONV_CHUNK = 32
PAGES_PER_STEP = 8
KT_SLAB = 256
VMEM_LIMIT = 56 * 1024 * 1024


def _dot(a, b):
    return jnp.dot(a, b, preferred_element_type=F32)


def _dot_nt(a, b):
    return lax.dot_general(a, b, (((1,), (1,)), ((), ())), preferred_element_type=F32)


def _silu(x):
    return x * jax.nn.sigmoid(x)


def _rms(x, w, eps):
    ms = jnp.mean(x * x, axis=-1, keepdims=True)
    return (x * lax.rsqrt(ms + eps)) * w


def _resident(shape):
    return pl.BlockSpec(shape, lambda *_: (0,) * len(shape), pipeline_mode=pl.Buffered(1))


def _resident_layer(stacked, layer):
    shape = stacked.shape[1:]
    return pl.BlockSpec((None,) + shape, lambda *_: (layer,) + (0,) * len(shape),
                        pipeline_mode=pl.Buffered(1))


def _smem():
    return pl.BlockSpec(memory_space=pltpu.SMEM)


def _inproj_kernel(x_ref, nw_ref, w_ref, u_ref, cg_ref, q_ref, k_ref, v_ref, ag_ref,
                   kb_ref, vb_ref, *, c_conv, d_att, scale):
    xb = _rms(x_ref[...], nw_ref[...], 1e-6).astype(BF16)

    def mm(c0, width):
        return _dot(xb, w_ref[:, c0:c0 + width])

    a = mm(0, c_conv)
    g = mm(c_conv, c_conv)
    u_ref[...] = a * jax.nn.sigmoid(g)
    cg_ref[...] = mm(2 * c_conv, c_conv)
    base = 3 * c_conv
    q_ref[...] = (mm(base, d_att) * scale).astype(BF16)
    k = mm(base + d_att, d_att)
    k_ref[...] = k
    kb_ref[...] = k.astype(BF16)
    v = mm(base + 2 * d_att, d_att)
    tm = v.shape[0]
    for h in range(N_HEADS):
        v_ref[pl.ds(h, tm, stride=N_HEADS), :] = v[:, h * HEAD_DV:(h + 1) * HEAD_DV]
    vb_ref[...] = v.astype(BF16)
    ag_ref[...] = mm(base + 3 * d_att, d_att)


def _inproj(x, norm_w, w_in_bf, layer, tm):
    rows, d = x.shape
    c_conv = d // 2
    d_att = d - c_conv
    row_blk = lambda w: pl.BlockSpec((tm, w), lambda i: (i, 0))
    shapes = [
        jax.ShapeDtypeStruct((rows, c_conv), F32),
        jax.ShapeDtypeStruct((rows, c_conv), F32),
        jax.ShapeDtypeStruct((rows, d_att), BF16),
        jax.ShapeDtypeStruct((rows, d_att), F32),
        jax.ShapeDtypeStruct((rows * N_HEADS, HEAD_DV), F32),
        jax.ShapeDtypeStruct((rows, d_att), F32),
        jax.ShapeDtypeStruct((rows, d_att), BF16),
        jax.ShapeDtypeStruct((rows, d_att), BF16),
    ]
    norm_rows = norm_w.reshape(norm_w.shape[0], 1, d)
    return pl.pallas_call(
        functools.partial(_inproj_kernel, c_conv=c_conv, d_att=d_att, scale=HEAD_DK ** -0.5),
        grid=(rows // tm,),
        in_specs=[row_blk(d), _resident_layer(norm_rows, layer), _resident_layer(w_in_bf, layer)],
        out_specs=[pl.BlockSpec((tm * s.shape[0] // rows, s.shape[1]), lambda i: (i, 0))
                   for s in shapes],
        out_shape=shapes,
        compiler_params=pltpu.CompilerParams(
            dimension_semantics=("arbitrary",), vmem_limit_bytes=VMEM_LIMIT),
        name="inproj",
    )(x, norm_rows, w_in_bf)


def _flash_kernel(lam_ref, q_ref, k_ref, v_ref, km_ref, vm_ref, bmeta_ref, bsub_ref, bdiag_ref,
                  o_ref, st_sc, vxt_sc, *, tile, n_q, layer):
    qi = pl.program_id(2)
    heads = st_sc.shape[0]
    cols = lambda hh: slice(hh * HEAD_DV, (hh + 1) * HEAD_DV)
    pad = jnp.zeros((META_PAD - N_META, HEAD_DV), BF16)
    both = lambda b: jnp.concatenate([b, b], axis=1)

    q2t, k_meta = [], []
    for hh in range(heads):
        qt = q_ref[:, cols(hh)].T
        row = lax.broadcasted_iota(jnp.int32, qt.shape, 0)
        zero = jnp.zeros_like(qt)
        q2t.append(jnp.concatenate([jnp.where(row < HEAD_DK, qt, zero),
                                    jnp.where(row >= HEAD_DK, qt, zero)], axis=1))
        k_meta.append(jnp.concatenate([km_ref[:, cols(hh)], pad], axis=0))

    @pl.when(qi == 0)
    def _():
        for hh in range(heads):
            v_meta = jnp.concatenate([vm_ref[:, cols(hh)], pad], axis=0)
            vxt_sc[hh, 0:HEAD_DV, 0:META_PAD] = v_meta.T
            vxt_sc[hh, 0:HEAD_DV, META_PAD:] = v_ref[:, cols(hh)].T
            vxt_sc[hh, HEAD_DV:, :] = jnp.ones((BF16_ROWS, vxt_sc.shape[2]), BF16)

    def variant(c):
        m = [None] * heads
        acc = [None] * heads

        def pass1(hh, ki):
            if ki < 0:
                s = _dot(k_meta[hh], q2t[hh]) + both(bmeta_ref[hh, 0 if c == 0 else 1])
                st_sc[hh, 0:META_PAD, :] = s
                m[hh] = jnp.max(s, axis=0, keepdims=True)
                return
            s = _dot(k_ref[ki * tile:(ki + 1) * tile, cols(hh)], q2t[hh])
            if ki == c:
                s = s + both(bdiag_ref[hh])
            elif ki == c - 1:
                s = s + both(bsub_ref[hh])
            lo = META_PAD + ki * tile
            st_sc[hh, lo:lo + tile, :] = s
            m[hh] = jnp.maximum(m[hh], jnp.max(s, axis=0, keepdims=True))

        def pass2(hh, ki):
            lo, width = (0, META_PAD) if ki < 0 else (META_PAD + ki * tile, tile)
            p = jnp.exp(st_sc[hh, lo:lo + width, :] - m[hh]).astype(BF16)
            d = _dot(vxt_sc[hh, :, lo:lo + width], p)
            acc[hh] = d if acc[hh] is None else acc[hh] + d

        for stage in range(heads + 1):
            for ki in range(-1, c + 1):
                if stage < heads:
                    pass1(stage, ki)
                if stage >= 1:
                    pass2(stage - 1, ki)
        for hh in range(heads):
            on = acc[hh][0:HEAD_DV] / acc[hh][HEAD_DV:HEAD_DV + 1]
            o_ref[:, cols(hh)] = (on[:, :tile] - lam_ref[layer] * on[:, tile:]).T

    for c in range(n_q):
        pl.when(qi == c)(functools.partial(variant, c))


def _flash(q, kb, vb, kb_side, vb_side, bmeta, bsub, bdiag, lam, layer, batch, seq):
    tile = ATT_TILE
    nq = seq // tile
    d_att = q.shape[1]
    hps = ATT_HEADS_PER_STEP
    width = hps * HEAD_DV
    meta = pl.BlockSpec((N_META, width), lambda b, h, qi: (0, h))
    per_head = lambda *dims: pl.BlockSpec((hps,) + dims, lambda b, h, qi: (h,) + (0,) * len(dims))
    return pl.pallas_call(
        functools.partial(_flash_kernel, tile=tile, n_q=nq, layer=layer),
        grid=(batch, N_HEADS // hps, nq),
        in_specs=[
            _smem(),
            pl.BlockSpec((tile, width), lambda b, h, qi: (b * nq + qi, h)),
            pl.BlockSpec((seq, width), lambda b, h, qi: (b, h)),
            pl.BlockSpec((seq, width), lambda b, h, qi: (b, h)),
            meta, meta,
            per_head(2, META_PAD, tile), per_head(tile, tile), per_head(tile, tile),
        ],
        out_specs=pl.BlockSpec((tile, width), lambda b, h, qi: (b * nq + qi, h)),
        out_shape=jax.ShapeDtypeStruct((batch * seq, d_att), F32),
        scratch_shapes=[pltpu.VMEM((hps, META_PAD + seq, 2 * tile), F32),
                        pltpu.VMEM((hps, HEAD_DV + BF16_ROWS, META_PAD + seq), BF16)],
        compiler_params=pltpu.CompilerParams(
            dimension_semantics=("arbitrary", "arbitrary", "arbitrary"),
            vmem_limit_bytes=VMEM_LIMIT),
        name="prompt_attention",
    )(lam, q, kb, vb, kb_side, vb_side, bmeta, bsub, bdiag)


def _meta_attn_kernel(lam_ref, q_ref, k_ref, v_ref, bias_ref, o_ref, *, layer):
    lam = lam_ref[layer]
    lane = lax.broadcasted_iota(jnp.int32, (q_ref.shape[0], HEAD_DV), 1)
    for h in range(N_HEADS):
        c0 = h * HEAD_DV
        bias = bias_ref[h]
        q = q_ref[:, c0:c0 + HEAD_DV]
        k = k_ref[:, c0:c0 + HEAD_DV]
        zero = jnp.zeros_like(q)
        ps = []
        for qc in (jnp.where(lane < HEAD_DK, q, zero), jnp.where(lane >= HEAD_DK, q, zero)):
            s = _dot_nt(qc, k) + bias
            e = jnp.exp(s - jnp.max(s, axis=-1, keepdims=True))
            ps.append(e / jnp.sum(e, axis=-1, keepdims=True))
        wgt = (ps[0] - lam * ps[1]).astype(BF16)
        o_ref[:, c0:c0 + HEAD_DV] = _dot(wgt, v_ref[:, c0:c0 + HEAD_DV])


def _meta_attn(q_side, kb_side, vb_side, bias, lam, layer):
    d_att = q_side.shape[1]
    top = pl.BlockSpec((N_META, d_att), lambda i: (0, 0))
    return pl.pallas_call(
        functools.partial(_meta_attn_kernel, layer=layer),
        grid=(1,),
        in_specs=[_smem(), top, top, top, pl.BlockSpec(bias.shape, lambda i: (0, 0, 0))],
        out_specs=top,
        out_shape=jax.ShapeDtypeStruct((N_META, d_att), F32),
        name="meta_attention",
    )(lam, q_side, kb_side, vb_side, bias)


def _decode_kernel(pt_ref, lam_ref, q_ref, *rest, pages, n_steps, layer):
    del pt_ref
    kp = rest[:pages]
    vp = rest[pages:2 * pages]
    kself_ref, vself_ref, blast_ref, bself_ref, o_ref, m_sc, l_sc, acc_sc = rest[2 * pages:]
    seq_id = pl.program_id(0)
    step_id = pl.program_id(1)
    rows = 2 * N_HEADS
    d_att = q_ref.shape[-1]
    row = lax.broadcasted_iota(jnp.int32, (rows, d_att), 0)
    lane = lax.broadcasted_iota(jnp.int32, (rows, d_att), 1)

    def own_row(ref):
        return jnp.sum(jnp.where(row == seq_id, ref[...].astype(F32), 0.0), axis=0, keepdims=True)

    def as_first_row(x):
        return jnp.where(row == 0, jnp.broadcast_to(x, (rows, d_att)), 0.0).astype(BF16)

    seg = lax.shift_right_logical(lane, int(math.log2(HEAD_DK)))
    qbd = jnp.where(seg == row, jnp.broadcast_to(own_row(q_ref), (rows, d_att)), 0.0).astype(BF16)

    @pl.when(step_id == 0)
    def _():
        m_sc[...] = jnp.full(m_sc.shape, NEG, F32)
        l_sc[...] = jnp.zeros(l_sc.shape, F32)
        acc_sc[...] = jnp.zeros(acc_sc.shape, F32)

    def update(s_parts, v_parts):
        m_prev = m_sc[...]
        m_new = m_prev
        for s in s_parts:
            m_new = jnp.maximum(m_new, jnp.max(s, axis=-1, keepdims=True))
        alpha = jnp.exp(m_prev - m_new)
        l_new = alpha * l_sc[...]
        acc = alpha * acc_sc[...]
        for s, v in zip(s_parts, v_parts):
            p = jnp.exp(s - m_new)
            l_new = l_new + jnp.sum(p, axis=-1, keepdims=True)
            acc = acc + _dot(p.astype(BF16), v)
        m_sc[...] = m_new
        l_sc[...] = l_new
        acc_sc[...] = acc

    is_last = step_id == n_steps - 1
    page_size = blast_ref.shape[1]
    s_parts = []
    v_parts = []
    for g in range(pages):
        s = _dot(qbd, kp[g][...].astype(BF16))
        if g == pages - 1:
            s = s + jnp.where(is_last, blast_ref[...], 0.0)
        s_parts.append(s)
        v_heads = [vp[g][pl.ds(h, page_size, stride=N_HEADS), :] for h in range(N_HEADS)]
        v_parts.append(jnp.concatenate(v_heads, axis=1).astype(BF16))
    update(s_parts, v_parts)

    @pl.when(is_last)
    def _():
        k_self = as_first_row(own_row(kself_ref))
        v_self = as_first_row(own_row(vself_ref))
        update([_dot_nt(qbd, k_self) + bself_ref[...]], [v_self])
        on = acc_sc[...] / l_sc[...]
        own = lax.shift_right_logical(lane, int(math.log2(HEAD_DV))) == lax.shift_right_logical(row, 1)
        comp0 = (row & 1) == 0
        sel = jnp.where(own, jnp.where(comp0, 1.0, -lam_ref[layer]), 0.0)
        o_ref[...] = jnp.sum(on * sel, axis=0, keepdims=True)


def _decode_attn(page_table, lam, q_side, kb_side, vb_side, cache_kt, cache_vr, layer, blast, bself):
    db, n_pages = page_table.shape
    d_att, page_size = cache_kt.shape[2], cache_kt.shape[3]
    pages = PAGES_PER_STEP
    n_steps = n_pages // pages
    rows = 2 * N_HEADS
    assert rows == BF16_ROWS and db <= rows and N_META == rows

    def page_spec(arr):
        return lambda g: pl.BlockSpec(
            (None, None) + arr.shape[2:], lambda b, s, pt: (layer, pt[b, s * pages + g], 0, 0))

    k_spec, v_spec = page_spec(cache_kt), page_spec(cache_vr)
    dec_rows = pl.BlockSpec((rows, d_att), lambda b, s, pt: (N_META // rows, 0))
    whole = lambda shape: pl.BlockSpec(shape, lambda b, s, pt: (0,) * len(shape))
    grid_spec = pltpu.PrefetchScalarGridSpec(
        num_scalar_prefetch=1,
        grid=(db, n_steps),
        in_specs=([_smem(), dec_rows]
                  + [k_spec(g) for g in range(pages)]
                  + [v_spec(g) for g in range(pages)]
                  + [dec_rows, dec_rows, whole((rows, page_size)), whole((rows, rows))]),
        out_specs=pl.BlockSpec((None, 1, d_att), lambda b, s, pt: (b, 0, 0)),
        scratch_shapes=[pltpu.VMEM((rows, 1), F32), pltpu.VMEM((rows, 1), F32),
                        pltpu.VMEM((rows, d_att), F32)],
    )
    return pl.pallas_call(
        functools.partial(_decode_kernel, pages=pages, n_steps=n_steps, layer=layer),
        grid_spec=grid_spec,
        out_shape=jax.ShapeDtypeStruct((db, 1, d_att), F32),
        compiler_params=pltpu.CompilerParams(
            dimension_semantics=("arbitrary", "arbitrary"), vmem_limit_bytes=VMEM_LIMIT),
        name="decode_attention",
    )(page_table, lam, q_side, *([cache_kt] * pages), *([cache_vr] * pages),
      kb_side, vb_side, blast, bself)


def _ln_swish(y, lnw, lnb):
    mu = jnp.mean(y, axis=-1, keepdims=True)
    yc = y - mu
    var = jnp.mean(yc * yc, axis=-1, keepdims=True)
    return _silu((yc * lax.rsqrt(var + 1e-5)) * lnw + lnb)


def _conv_kernel(u_ref, halo_ref, us_ref, w_ref, b_ref, lnw_ref, lnb_ref, z_ref, full_sc,
                 shift_sc, *, tm):
    ti = pl.program_id(1)
    c = u_ref.shape[1]
    prefix = jnp.concatenate([jnp.zeros((HALO - N_META, c), F32), us_ref[0:N_META, :]], axis=0)
    full_sc[0:HALO, :] = jnp.where(ti == 0, prefix, halo_ref[...])
    full_sc[HALO:HALO + tm, :] = u_ref[...]
    span = shift_sc.shape[1]
    for r in range(1, SUBLANES):
        shift_sc[r - 1] = full_sc[r:r + span, :]
    first = HALO - (CONV_K - 1)

    for r0 in range(0, tm, CONV_CHUNK):
        acc = None
        for k in range(CONV_K):
            r = (first + k) % SUBLANES
            base = r0 + first + k - r
            if r == 0:
                window = full_sc[base:base + CONV_CHUNK, :]
            else:
                window = shift_sc[r - 1, base:base + CONV_CHUNK, :]
            term = window * w_ref[k:k + 1, :]
            acc = term if acc is None else acc + term
        z = _ln_swish(acc + b_ref[...], lnw_ref[...], lnb_ref[...])
        z_ref[r0:r0 + CONV_CHUNK, :] = z.astype(BF16)


def _conv_params(conv_w, conv_b, ln_w, ln_b, layer):
    depth, c = conv_b.shape
    vec = lambda a: a.reshape(depth, 1, c)
    arrays = (conv_w, vec(conv_b), vec(ln_w), vec(ln_b))
    return arrays, [_resident_layer(a, layer) for a in arrays]


def _conv(u, u_side, conv_w, conv_b, ln_w, ln_b, layer, batch, seq):
    tm = ROW_TILE
    nt = seq // tm
    c = u.shape[1]
    ratio = tm // HALO
    params, param_specs = _conv_params(conv_w, conv_b, ln_w, ln_b, layer)
    return pl.pallas_call(
        functools.partial(_conv_kernel, tm=tm),
        grid=(batch, nt),
        in_specs=[
            pl.BlockSpec((tm, c), lambda b, t: (b * nt + t, 0)),
            pl.BlockSpec((HALO, c), lambda b, t: (jnp.maximum((b * nt + t) * ratio - 1, 0), 0)),
            _resident(u_side.shape),
        ] + param_specs,
        out_specs=pl.BlockSpec((tm, c), lambda b, t: (b * nt + t, 0)),
        out_shape=jax.ShapeDtypeStruct((batch * seq, c), BF16),
        scratch_shapes=[pltpu.VMEM((HALO + tm, c), F32),
                        pltpu.VMEM((SUBLANES - 1, HALO + tm - SUBLANES, c), F32)],
        compiler_params=pltpu.CompilerParams(
            dimension_semantics=("arbitrary", "arbitrary"), vmem_limit_bytes=VMEM_LIMIT),
        name="conv",
    )(u, u, u_side, *params)


def _side_conv_kernel(us_ref, st_ref, w_ref, b_ref, lnw_ref, lnb_ref, z_ref, ns_ref, *, n_dec):
    c = us_ref.shape[1]
    hist_len = CONV_K - 1
    full = jnp.concatenate([jnp.zeros((hist_len, c), F32), us_ref[0:N_META, :]], axis=0)
    acc = full[0:N_META] * w_ref[0:1, :]
    for k in range(1, CONV_K):
        acc = acc + full[k:k + N_META] * w_ref[k:k + 1, :]
    z_ref[0:N_META, :] = _ln_swish(acc + b_ref[...], lnw_ref[...], lnb_ref[...])
    for n in range(n_dec):
        u_n = us_ref[N_META + n:N_META + n + 1, :]
        y = (jnp.sum(st_ref[n] * w_ref[0:hist_len, :], axis=0, keepdims=True)
             + u_n * w_ref[hist_len:CONV_K, :])
        z_ref[N_META + n:N_META + n + 1, :] = _ln_swish(y + b_ref[...], lnw_ref[...], lnb_ref[...])
        ns_ref[n, 0:hist_len - 1, :] = st_ref[n, 1:hist_len, :]
        ns_ref[n, hist_len - 1:hist_len, :] = u_n
    n_pad = z_ref.shape[0] - N_META - n_dec
    z_ref[N_META + n_dec:, :] = jnp.zeros((n_pad, c), F32)


def _side_conv(u_side, state_conv, conv_w, conv_b, ln_w, ln_b, layer):
    rows, c = u_side.shape
    n_dec = state_conv.shape[1]
    params, param_specs = _conv_params(conv_w, conv_b, ln_w, ln_b, layer)
    return pl.pallas_call(
        functools.partial(_side_conv_kernel, n_dec=n_dec),
        grid=(1,),
        in_specs=[_resident(u_side.shape), _resident_layer(state_conv, layer)] + param_specs,
        out_specs=[pl.BlockSpec((rows, c), lambda i: (0, 0)),
                   pl.BlockSpec(state_conv.shape[1:], lambda i: (0, 0, 0))],
        out_shape=[jax.ShapeDtypeStruct((rows, c), F32),
                   jax.ShapeDtypeStruct(state_conv.shape[1:], F32)],
        name="side_conv",
    )(u_side, state_conv, *params)


def _out_kernel(z_ref, cg_ref, o_ref, ag_ref, h_ref, wpw_ref, sub_ref, wout_ref, *rest,
                post_scale, final_norm):
    conv_y = _dot(z_ref[...].astype(BF16), wpw_ref[...]) * _silu(cg_ref[...])
    sub = sub_ref[...]
    atts = []
    for h in range(N_HEADS):
        c0 = h * HEAD_DV
        att = (_rms(o_ref[:, c0:c0 + HEAD_DV], sub, 1e-5) * post_scale) * _silu(ag_ref[:, c0:c0 + HEAD_DV])
        atts.append(att.astype(BF16))
    mixed = jnp.concatenate([conv_y.astype(BF16)] + atts, axis=1)
    h_new = h_ref[...] + _dot(mixed, wout_ref[...])
    if final_norm:
        fw_ref, y_ref = rest
        y_ref[...] = _rms(h_new, fw_ref[...], 1e-6)
    else:
        (y_ref,) = rest
        y_ref[...] = h_new


def _out_proj(z, cg, o, ag, h, w_pw2_bf, subln_w, w_out_bf, layer, lam_init, tm, final_norm_w=None):
    rows, d = h.shape
    c_conv = z.shape[1]
    d_att = o.shape[1]
    blk = lambda w: pl.BlockSpec((tm, w), lambda i: (i, 0))
    sub_rows = subln_w.reshape(subln_w.shape[0], 1, HEAD_DV)
    in_specs = [blk(c_conv), blk(c_conv), blk(d_att), blk(d_att), blk(d),
                _resident_layer(w_pw2_bf, layer), _resident_layer(sub_rows, layer),
                _resident_layer(w_out_bf, layer)]
    args = [z, cg, o, ag, h, w_pw2_bf, sub_rows, w_out_bf]
    if final_norm_w is not None:
        in_specs.append(_resident((1, d)))
        args.append(final_norm_w.reshape(1, d))
    return pl.pallas_call(
        functools.partial(_out_kernel, post_scale=1.0 - lam_init,
                          final_norm=final_norm_w is not None),
        grid=(rows // tm,),
        in_specs=in_specs,
        out_specs=blk(d),
        out_shape=jax.ShapeDtypeStruct((rows, d), F32),
        compiler_params=pltpu.CompilerParams(
            dimension_semantics=("arbitrary",), vmem_limit_bytes=VMEM_LIMIT),
        name="out_proj",
    )(*args)


def _kt_kernel(*refs, depth):
    k_refs, km_refs, o_ref = refs[:depth], refs[depth:2 * depth], refs[2 * depth]
    layer = pl.program_id(0)
    for j in range(depth):
        @pl.when(layer == j)
        def _(j=j):
            o_ref[:, 0:N_META] = km_refs[j][0:N_META, :].T
            o_ref[:, N_META:] = k_refs[j][...].T


def _key_output(k_layers, k_side_layers, batch, seq):
    depth = len(k_layers)
    d_att = k_layers[0].shape[1]
    n_slab = d_att // KT_SLAB
    side_rows = k_side_layers[0].shape[0]

    def when_layer(j, cur, last):
        return lambda l, b, s: tuple(
            jnp.where(l < j, 0, jnp.where(l > j, e, c)) for c, e in zip(cur(b, s), last))

    k_specs = [pl.BlockSpec((seq, KT_SLAB), when_layer(j, lambda b, s: (b, s), (batch - 1, n_slab - 1)))
               for j in range(depth)]
    km_specs = [pl.BlockSpec((side_rows, KT_SLAB), when_layer(j, lambda b, s: (0, s), (0, n_slab - 1)))
                for j in range(depth)]
    return pl.pallas_call(
        functools.partial(_kt_kernel, depth=depth),
        grid=(depth, batch, n_slab),
        in_specs=k_specs + km_specs,
        out_specs=pl.BlockSpec((None, None, KT_SLAB, N_META + seq), lambda l, b, s: (l, b, s, 0)),
        out_shape=jax.ShapeDtypeStruct((depth, batch, d_att, N_META + seq), F32),
        compiler_params=pltpu.CompilerParams(
            dimension_semantics=("arbitrary", "arbitrary", "arbitrary"),
            vmem_limit_bytes=VMEM_LIMIT),
        name="key_output",
    )(*k_layers, *k_side_layers)


def _bias_by_distance(rel_bias, n):
    d = jnp.arange(n)
    max_exact = N_BUCKETS // 2
    nf = jnp.maximum(d, 1).astype(F32)
    large = max_exact + (jnp.log(nf / max_exact) / math.log(MAX_DIST / max_exact)
                         * (N_BUCKETS - max_exact)).astype(jnp.int32)
    bucket = jnp.where(d < max_exact, d, jnp.minimum(large, N_BUCKETS - 1))
    t = rel_bias[bucket] - rel_bias[N_BUCKETS - 1]
    return t.T.astype(F32)


def _toeplitz(t, offset, rows, cols):
    heads = t.shape[0]
    n_diag = rows + cols - 1
    lo = offset - (cols - 1)
    n_neg = max(0, -lo)
    diag = jnp.concatenate([jnp.full((heads, n_neg), NEG, F32),
                            t[:, lo + n_neg:lo + n_diag]], axis=1)
    period = jnp.concatenate([diag[:, ::-1], jnp.zeros((heads, 1), F32)], axis=1)
    skew = jnp.tile(period, (1, rows))[:, :rows * n_diag].reshape(heads, rows, n_diag)
    return skew[:, :, rows - 1:rows - 1 + cols]


def _meta_bias(t, tile):
    near = jnp.swapaxes(_toeplitz(t, N_META, tile, N_META), 1, 2)
    far = jnp.zeros_like(near)
    masked = jnp.full((near.shape[0], META_PAD - N_META, tile), NEG, F32)
    return jnp.stack([jnp.concatenate([near, masked], axis=1),
                      jnp.concatenate([far, masked], axis=1)], axis=1)


def kernel(x_prompt, x_sample, cache_k, cache_v, state_conv, page_table, meta_tokens, rel_bias,
           norm_w, w_in, conv_w, conv_b, conv_ln_w, conv_ln_b, w_pw2, lambda_q1, lambda_k1,
           lambda_q2, lambda_k2, subln_w, w_out, final_norm_w):
    batch, seq, d = x_prompt.shape
    depth = w_in.shape[0]
    db = x_sample.shape[0]
    c_conv = d // 2
    d_att = d - c_conv
    n_pool, page_size = cache_k.shape[1], cache_k.shape[2]
    tile = ATT_TILE
    rows2 = 2 * N_HEADS

    cache_kt = jnp.transpose(cache_k, (0, 1, 3, 4, 5, 2)).reshape(depth, n_pool, d_att, page_size)
    cache_vr = cache_v.reshape(depth, n_pool, page_size * N_HEADS, HEAD_DV)

    t = _bias_by_distance(rel_bias, 2 * tile + N_META)
    bdiag = jnp.swapaxes(_toeplitz(t, 0, tile, tile), 1, 2)
    bsub = jnp.swapaxes(_toeplitz(t, tile, tile, tile), 1, 2)
    bmeta = _meta_bias(t, tile)
    bias_meta = _toeplitz(t, 0, N_META, N_META)
    t_rows = jnp.repeat(t, 2, axis=0)
    blast = t_rows[:, page_size:0:-1]
    bself = jnp.where(jnp.arange(rows2)[None, :] == 0, t_rows[:, 0:1], NEG)

    lam_inits = [0.8 - 0.6 * math.exp(-0.3 * l) for l in range(depth)]
    f32 = lambda a: a.astype(F32)
    lam = (jnp.exp(jnp.sum(f32(lambda_q1) * f32(lambda_k1), axis=-1))
           - jnp.exp(jnp.sum(f32(lambda_q2) * f32(lambda_k2), axis=-1))
           + jnp.asarray(lam_inits, F32))

    hp = x_prompt.reshape(batch * seq, d)
    side = jnp.concatenate([meta_tokens.astype(F32), x_sample.reshape(db, d),
                            jnp.zeros((SIDE_ROWS - N_META - db, d), F32)], axis=0)
    w_in_bf = w_in.astype(BF16)
    w_pw2_bf = w_pw2.astype(BF16)
    w_out_bf = w_out.astype(BF16)
    conv_p = (conv_w, conv_b, conv_ln_w, conv_ln_b)
    k_l, ksd_l, vp_l, cp_l, ks_l, vs_l, cs_l = [], [], [], [], [], [], []
    for l in range(depth):
        fin = final_norm_w if l == depth - 1 else None

        us, cgs, qs, ks, vs, ags, kbs, vbs = _inproj(side, norm_w, w_in_bf, l, SIDE_ROWS)
        z_side, new_state = _side_conv(us, state_conv, *conv_p, l)
        o_meta = _meta_attn(qs, kbs, vbs, bias_meta, lam, l)
        o_dec = _decode_attn(page_table, lam, qs, kbs, vbs, cache_kt, cache_vr, l, blast, bself)
        o_side = jnp.concatenate([o_meta, o_dec[:, 0],
                                  jnp.zeros((SIDE_ROWS - N_META - db, d_att), F32)], axis=0)

        u, cg, q, k, v, ag, kb, vb = _inproj(hp, norm_w, w_in_bf, l, ROW_TILE)
        z = _conv(u, us, *conv_p, l, batch, seq)
        o = _flash(q, kb, vb, kbs, vbs, bmeta, bsub, bdiag, lam, l, batch, seq)
        hp = _out_proj(z, cg, o, ag, hp, w_pw2_bf, subln_w, w_out_bf, l, lam_inits[l], ROW_TILE, fin)
        side = _out_proj(z_side, cgs, o_side, ags, side, w_pw2_bf, subln_w, w_out_bf, l,
                         lam_inits[l], SIDE_ROWS, fin)

        k_l.append(k)
        ksd_l.append(ks)
        vs = vs.reshape(SIDE_ROWS, N_HEADS, HEAD_DV)
        v_meta = jnp.broadcast_to(vs[:N_META][None], (batch, N_META, N_HEADS, HEAD_DV))
        vp_l.append(jnp.concatenate([v_meta, v.reshape(batch, seq, N_HEADS, HEAD_DV)], axis=1))
        cp_l.append(u.reshape(batch, seq, c_conv)[:, seq - (CONV_K - 1):])
        ks_l.append(ks[N_META:N_META + db])
        vs_l.append(vs[N_META:N_META + db])
        cs_l.append(new_state)

    lp = seq + N_META
    kt = _key_output(k_l, ksd_l, batch, seq)
    new_k = jnp.transpose(kt.reshape(depth, batch, N_HEADS, 2, HEAD_DK, lp), (0, 1, 5, 2, 3, 4))
    return (hp.reshape(batch, seq, d),
            side[N_META:N_META + db].reshape(db, 1, d),
            new_k,
            jnp.stack(vp_l).reshape(depth, batch, lp, N_HEADS, HEAD_DV),
            jnp.stack(cp_l),
            jnp.stack(ks_l).reshape(depth, db, 1, N_HEADS, 2, HEAD_DK),
            jnp.stack(vs_l).reshape(depth, db, 1, N_HEADS, HEAD_DV),
            jnp.stack(cs_l))
```

```python
import functools
import math

import jax
import jax.numpy as jnp
from jax import lax
from jax.experimental import pallas as pl
from jax.experimental.pallas import tpu as pltpu

F32 = jnp.float32
BF16 = jnp.bfloat16

N_META = 16
N_HEADS = 8
HEAD_DV = 128
HEAD_DK = 64
CONV_K = 31
N_BUCKETS = 32
MAX_DIST = 128
NEG = -1e30
SUBLANES = 8
BF16_ROWS = 16
SIDE_ROWS = 32
HALO = 32
ROW_TILE = 256
OUT_TILE = 512
ATT_TILE = 512
ATT_HEADS_PER_STEP = 2
META_PAD = 128
BIAS_BLOCK = MAX_DIST
CONV_CHUNK = 32
PAGES_PER_STEP = 8
KT_SLAB = 256
VMEM_LIMIT = 56 * 1024 * 1024


def _dot(a, b):
    return jnp.dot(a, b, preferred_element_type=F32)


def _dot_nt(a, b):
    return lax.dot_general(a, b, (((1,), (1,)), ((), ())), preferred_element_type=F32)


def _silu(x):
    return x * jax.nn.sigmoid(x)


def _rms(x, w, eps):
    ms = jnp.mean(x * x, axis=-1, keepdims=True)
    return (x * lax.rsqrt(ms + eps)) * w


def _resident(shape):
    return pl.BlockSpec(shape, lambda *_: (0,) * len(shape), pipeline_mode=pl.Buffered(1))


def _resident_layer(stacked, layer):
    shape = stacked.shape[1:]
    return pl.BlockSpec((None,) + shape, lambda *_: (layer,) + (0,) * len(shape),
                        pipeline_mode=pl.Buffered(1))


def _smem():
    return pl.BlockSpec(memory_space=pltpu.SMEM)


def _inproj_kernel(x_ref, nw_ref, w_ref, u_ref, cg_ref, q_ref, k_ref, v_ref, ag_ref,
                   kb_ref, vb_ref, *, c_conv, d_att, scale):
    xb = _rms(x_ref[...], nw_ref[...], 1e-6).astype(BF16)

    def mm(c0, width):
        return _dot(xb, w_ref[:, c0:c0 + width])

    a = mm(0, c_conv)
    g = mm(c_conv, c_conv)
    u_ref[...] = a * jax.nn.sigmoid(g)
    cg_ref[...] = mm(2 * c_conv, c_conv)
    base = 3 * c_conv
    q_ref[...] = (mm(base, d_att) * scale).astype(BF16)
    k = mm(base + d_att, d_att)
    k_ref[...] = k
    kb_ref[...] = k.astype(BF16)
    v = mm(base + 2 * d_att, d_att)
    tm = v.shape[0]
    for h in range(N_HEADS):
        v_ref[pl.ds(h, tm, stride=N_HEADS), :] = v[:, h * HEAD_DV:(h + 1) * HEAD_DV]
    vb_ref[...] = v.astype(BF16)
    ag_ref[...] = mm(base + 3 * d_att, d_att)


def _inproj(x, norm_w, w_in_bf, layer, tm):
    rows, d = x.shape
    c_conv = d // 2
    d_att = d - c_conv
    row_blk = lambda w: pl.BlockSpec((tm, w), lambda i: (i, 0))
    shapes = [
        jax.ShapeDtypeStruct((rows, c_conv), F32),
        jax.ShapeDtypeStruct((rows, c_conv), F32),
        jax.ShapeDtypeStruct((rows, d_att), BF16),
        jax.ShapeDtypeStruct((rows, d_att), F32),
        jax.ShapeDtypeStruct((rows * N_HEADS, HEAD_DV), F32),
        jax.ShapeDtypeStruct((rows, d_att), F32),
        jax.ShapeDtypeStruct((rows, d_att), BF16),
        jax.ShapeDtypeStruct((rows, d_att), BF16),
    ]
    norm_rows = norm_w.reshape(norm_w.shape[0], 1, d)
    return pl.pallas_call(
        functools.partial(_inproj_kernel, c_conv=c_conv, d_att=d_att, scale=HEAD_DK ** -0.5),
        grid=(rows // tm,),
        in_specs=[row_blk(d), _resident_layer(norm_rows, layer), _resident_layer(w_in_bf, layer)],
        out_specs=[pl.BlockSpec((tm * s.shape[0] // rows, s.shape[1]), lambda i: (i, 0))
                   for s in shapes],
        out_shape=shapes,
        compiler_params=pltpu.CompilerParams(
            dimension_semantics=("arbitrary",), vmem_limit_bytes=VMEM_LIMIT),
        name="inproj",
    )(x, norm_rows, w_in_bf)


def _flash_kernel(lam_ref, q_ref, k_ref, v_ref, km_ref, vm_ref, bmeta_ref, bnear_ref,
                  o_ref, st_sc, vxt_sc, *, tile, n_q, layer):
    qi = pl.program_id(2)
    heads = st_sc.shape[0]
    cols = lambda hh: slice(hh * HEAD_DV, (hh + 1) * HEAD_DV)
    pad = jnp.zeros((META_PAD - N_META, HEAD_DV), BF16)
    both = lambda b: jnp.concatenate([b, b], axis=1)

    q2t, k_meta = [], []
    for hh in range(heads):
        qt = q_ref[:, cols(hh)].T
        row = lax.broadcasted_iota(jnp.int32, qt.shape, 0)
        zero = jnp.zeros_like(qt)
        q2t.append(jnp.concatenate([jnp.where(row < HEAD_DK, qt, zero),
                                    jnp.where(row >= HEAD_DK, qt, zero)], axis=1))
        k_meta.append(jnp.concatenate([km_ref[:, cols(hh)], pad], axis=0))

    @pl.when(qi == 0)
    def _():
        for hh in range(heads):
            v_meta = jnp.concatenate([vm_ref[:, cols(hh)], pad], axis=0)
            vxt_sc[hh, 0:HEAD_DV, 0:META_PAD] = v_meta.T
            vxt_sc[hh, 0:HEAD_DV, META_PAD:] = v_ref[:, cols(hh)].T
            vxt_sc[hh, HEAD_DV:, :] = jnp.ones((BF16_ROWS, vxt_sc.shape[2]), BF16)

    def near_bias(hh, offset):
        nb = tile // BIAS_BLOCK
        neg = jnp.full((BIAS_BLOCK, BIAS_BLOCK), NEG, F32)
        zero = jnp.zeros((BIAS_BLOCK, BIAS_BLOCK), F32)
        rows = []
        for bj in range(nb):
            blocks = []
            for bi in range(nb):
                lead = offset // BIAS_BLOCK + bi - bj
                blocks.append(neg if lead < 0 else bnear_ref[hh, lead] if lead < 2 else zero)
            rows.append(jnp.concatenate(blocks, axis=1))
        return jnp.concatenate(rows, axis=0)

    def variant(c):
        m = [None] * heads
        acc = [None] * heads

        def pass1(hh, ki):
            if ki < 0:
                s = _dot(k_meta[hh], q2t[hh]) + both(bmeta_ref[hh, 0 if c == 0 else 1])
                st_sc[hh, 0:META_PAD, :] = s
                m[hh] = jnp.max(s, axis=0, keepdims=True)
                return
            s = _dot(k_ref[ki * tile:(ki + 1) * tile, cols(hh)], q2t[hh])
            if ki == c:
                s = s + both(near_bias(hh, 0))
            elif ki == c - 1:
                s = s + both(near_bias(hh, tile))
            lo = META_PAD + ki * tile
            st_sc[hh, lo:lo + tile, :] = s
            m[hh] = jnp.maximum(m[hh], jnp.max(s, axis=0, keepdims=True))

        def pass2(hh, ki):
            lo, width = (0, META_PAD) if ki < 0 else (META_PAD + ki * tile, tile)
            p = jnp.exp(st_sc[hh, lo:lo + width, :] - m[hh]).astype(BF16)
            d = _dot(vxt_sc[hh, :, lo:lo + width], p)
            acc[hh] = d if acc[hh] is None else acc[hh] + d

        for stage in range(heads + 1):
            for ki in range(-1, c + 1):
                if stage < heads:
                    pass1(stage, ki)
                if stage >= 1:
                    pass2(stage - 1, ki)
        for hh in range(heads):
            on = acc[hh][0:HEAD_DV] / acc[hh][HEAD_DV:HEAD_DV + 1]
            o_ref[:, cols(hh)] = (on[:, :tile] - lam_ref[layer] * on[:, tile:]).T

    for c in range(n_q):
        pl.when(qi == c)(functools.partial(variant, c))


def _flash(q, kb, vb, kb_side, vb_side, bmeta, bnear, lam, layer, batch, seq):
    tile = ATT_TILE
    nq = seq // tile
    d_att = q.shape[1]
    hps = ATT_HEADS_PER_STEP
    width = hps * HEAD_DV
    meta = pl.BlockSpec((N_META, width), lambda b, h, qi: (0, h))
    per_head = lambda *dims: pl.BlockSpec((hps,) + dims, lambda b, h, qi: (h,) + (0,) * len(dims))
    return pl.pallas_call(
        functools.partial(_flash_kernel, tile=tile, n_q=nq, layer=layer),
        grid=(batch, N_HEADS // hps, nq),
        in_specs=[
            _smem(),
            pl.BlockSpec((tile, width), lambda b, h, qi: (b * nq + qi, h)),
            pl.BlockSpec((seq, width), lambda b, h, qi: (b, h)),
            pl.BlockSpec((seq, width), lambda b, h, qi: (b, h)),
            meta, meta,
            per_head(2, META_PAD, tile), per_head(2, BIAS_BLOCK, BIAS_BLOCK),
        ],
        out_specs=pl.BlockSpec((tile, width), lambda b, h, qi: (b * nq + qi, h)),
        out_shape=jax.ShapeDtypeStruct((batch * seq, d_att), F32),
        scratch_shapes=[pltpu.VMEM((hps, META_PAD + seq, 2 * tile), F32),
                        pltpu.VMEM((hps, HEAD_DV + BF16_ROWS, META_PAD + seq), BF16)],
        compiler_params=pltpu.CompilerParams(
            dimension_semantics=("arbitrary", "arbitrary", "arbitrary"),
            vmem_limit_bytes=VMEM_LIMIT),
        name="prompt_attention",
    )(lam, q, kb, vb, kb_side, vb_side, bmeta, bnear)


def _meta_attn_kernel(lam_ref, q_ref, k_ref, v_ref, bias_ref, o_ref, *, layer):
    lam = lam_ref[layer]
    lane = lax.broadcasted_iota(jnp.int32, (q_ref.shape[0], HEAD_DV), 1)
    for h in range(N_HEADS):
        c0 = h * HEAD_DV
        bias = bias_ref[h]
        q = q_ref[:, c0:c0 + HEAD_DV]
        k = k_ref[:, c0:c0 + HEAD_DV]
        zero = jnp.zeros_like(q)
        ps = []
        for qc in (jnp.where(lane < HEAD_DK, q, zero), jnp.where(lane >= HEAD_DK, q, zero)):
            s = _dot_nt(qc, k) + bias
            e = jnp.exp(s - jnp.max(s, axis=-1, keepdims=True))
            ps.append(e / jnp.sum(e, axis=-1, keepdims=True))
        wgt = (ps[0] - lam * ps[1]).astype(BF16)
        o_ref[:, c0:c0 + HEAD_DV] = _dot(wgt, v_ref[:, c0:c0 + HEAD_DV])


def _meta_attn(q_side, kb_side, vb_side, bias, lam, layer):
    d_att = q_side.shape[1]
    top = pl.BlockSpec((N_META, d_att), lambda i: (0, 0))
    return pl.pallas_call(
        functools.partial(_meta_attn_kernel, layer=layer),
        grid=(1,),
        in_specs=[_smem(), top, top, top, pl.BlockSpec(bias.shape, lambda i: (0, 0, 0))],
        out_specs=top,
        out_shape=jax.ShapeDtypeStruct((N_META, d_att), F32),
        name="meta_attention",
    )(lam, q_side, kb_side, vb_side, bias)


def _decode_kernel(pt_ref, lam_ref, q_ref, *rest, pages, n_steps, layer):
    del pt_ref
    kp = rest[:pages]
    vp = rest[pages:2 * pages]
    kself_ref, vself_ref, blast_ref, bself_ref, o_ref, m_sc, l_sc, acc_sc = rest[2 * pages:]
    seq_id = pl.program_id(0)
    step_id = pl.program_id(1)
    rows = 2 * N_HEADS
    d_att = q_ref.shape[-1]
    row = lax.broadcasted_iota(jnp.int32, (rows, d_att), 0)
    lane = lax.broadcasted_iota(jnp.int32, (rows, d_att), 1)

    def own_row(ref):
        return jnp.sum(jnp.where(row == seq_id, ref[...].astype(F32), 0.0), axis=0, keepdims=True)

    def as_first_row(x):
        return jnp.where(row == 0, jnp.broadcast_to(x, (rows, d_att)), 0.0).astype(BF16)

    seg = lax.shift_right_logical(lane, int(math.log2(HEAD_DK)))
    qbd = jnp.where(seg == row, jnp.broadcast_to(own_row(q_ref), (rows, d_att)), 0.0).astype(BF16)

    @pl.when(step_id == 0)
    def _():
        m_sc[...] = jnp.full(m_sc.shape, NEG, F32)
        l_sc[...] = jnp.zeros(l_sc.shape, F32)
        acc_sc[...] = jnp.zeros(acc_sc.shape, F32)

    def update(s_parts, v_parts):
        m_prev = m_sc[...]
        m_new = m_prev
        for s in s_parts:
            m_new = jnp.maximum(m_new, jnp.max(s, axis=-1, keepdims=True))
        alpha = jnp.exp(m_prev - m_new)
        l_new = alpha * l_sc[...]
        acc = alpha * acc_sc[...]
        for s, v in zip(s_parts, v_parts):
            p = jnp.exp(s - m_new)
            l_new = l_new + jnp.sum(p, axis=-1, keepdims=True)
            acc = acc + _dot(p.astype(BF16), v)
        m_sc[...] = m_new
        l_sc[...] = l_new
        acc_sc[...] = acc

    is_last = step_id == n_steps - 1
    page_size = blast_ref.shape[1]
    s_parts = []
    v_parts = []
    for g in range(pages):
        s = _dot(qbd, kp[g][...].astype(BF16))
        if g == pages - 1:
            s = s + jnp.where(is_last, blast_ref[...], 0.0)
        s_parts.append(s)
        v_heads = [vp[g][pl.ds(h, page_size, stride=N_HEADS), :] for h in range(N_HEADS)]
        v_parts.append(jnp.concatenate(v_heads, axis=1).astype(BF16))
    update(s_parts, v_parts)

    @pl.when(is_last)
    def _():
        k_self = as_first_row(own_row(kself_ref))
        v_self = as_first_row(own_row(vself_ref))
        update([_dot_nt(qbd, k_self) + bself_ref[...]], [v_self])
        on = acc_sc[...] / l_sc[...]
        own = lax.shift_right_logical(lane, int(math.log2(HEAD_DV))) == lax.shift_right_logical(row, 1)
        comp0 = (row & 1) == 0
        sel = jnp.where(own, jnp.where(comp0, 1.0, -lam_ref[layer]), 0.0)
        o_ref[...] = jnp.sum(on * sel, axis=0, keepdims=True)


def _decode_attn(page_table, lam, q_side, kb_side, vb_side, cache_kt, cache_vr, layer, blast, bself):
    db, n_pages = page_table.shape
    d_att, page_size = cache_kt.shape[2], cache_kt.shape[3]
    pages = PAGES_PER_STEP
    n_steps = n_pages // pages
    rows = 2 * N_HEADS
    assert rows == BF16_ROWS and db <= rows and N_META == rows

    def page_spec(arr):
        return lambda g: pl.BlockSpec(
            (None, None) + arr.shape[2:], lambda b, s, pt: (layer, pt[b, s * pages + g], 0, 0))

    k_spec, v_spec = page_spec(cache_kt), page_spec(cache_vr)
    dec_rows = pl.BlockSpec((rows, d_att), lambda b, s, pt: (N_META // rows, 0))
    whole = lambda shape: pl.BlockSpec(shape, lambda b, s, pt: (0,) * len(shape))
    grid_spec = pltpu.PrefetchScalarGridSpec(
        num_scalar_prefetch=1,
        grid=(db, n_steps),
        in_specs=([_smem(), dec_rows]
                  + [k_spec(g) for g in range(pages)]
                  + [v_spec(g) for g in range(pages)]
                  + [dec_rows, dec_rows, whole((rows, page_size)), whole((rows, rows))]),
        out_specs=pl.BlockSpec((None, 1, d_att), lambda b, s, pt: (b, 0, 0)),
        scratch_shapes=[pltpu.VMEM((rows, 1), F32), pltpu.VMEM((rows, 1), F32),
                        pltpu.VMEM((rows, d_att), F32)],
    )
    return pl.pallas_call(
        functools.partial(_decode_kernel, pages=pages, n_steps=n_steps, layer=layer),
        grid_spec=grid_spec,
        out_shape=jax.ShapeDtypeStruct((db, 1, d_att), F32),
        compiler_params=pltpu.CompilerParams(
            dimension_semantics=("arbitrary", "arbitrary"), vmem_limit_bytes=VMEM_LIMIT),
        name="decode_attention",
    )(page_table, lam, q_side, *([cache_kt] * pages), *([cache_vr] * pages),
      kb_side, vb_side, blast, bself)


def _ln_swish(y, lnw, lnb):
    mu = jnp.mean(y, axis=-1, keepdims=True)
    yc = y - mu
    var = jnp.mean(yc * yc, axis=-1, keepdims=True)
    return _silu((yc * lax.rsqrt(var + 1e-5)) * lnw + lnb)


def _conv_kernel(u_ref, halo_ref, us_ref, w_ref, b_ref, lnw_ref, lnb_ref, z_ref, full_sc,
                 shift_sc, *, tm):
    ti = pl.program_id(1)
    c = u_ref.shape[1]
    prefix = jnp.concatenate([jnp.zeros((HALO - N_META, c), F32), us_ref[0:N_META, :]], axis=0)
    full_sc[0:HALO, :] = jnp.where(ti == 0, prefix, halo_ref[...])
    full_sc[HALO:HALO + tm, :] = u_ref[...]
    span = shift_sc.shape[1]
    for r in range(1, SUBLANES):
        shift_sc[r - 1] = full_sc[r:r + span, :]
    first = HALO - (CONV_K - 1)

    for r0 in range(0, tm, CONV_CHUNK):
        acc = None
        for k in range(CONV_K):
            r = (first + k) % SUBLANES
            base = r0 + first + k - r
            if r == 0:
                window = full_sc[base:base + CONV_CHUNK, :]
            else:
                window = shift_sc[r - 1, base:base + CONV_CHUNK, :]
            term = window * w_ref[k:k + 1, :]
            acc = term if acc is None else acc + term
        z = _ln_swish(acc + b_ref[...], lnw_ref[...], lnb_ref[...])
        z_ref[r0:r0 + CONV_CHUNK, :] = z.astype(BF16)


def _conv_params(conv_w, conv_b, ln_w, ln_b, layer):
    depth, c = conv_b.shape
    vec = lambda a: a.reshape(depth, 1, c)
    arrays = (conv_w, vec(conv_b), vec(ln_w), vec(ln_b))
    return arrays, [_resident_layer(a, layer) for a in arrays]


def _conv(u, u_side, conv_w, conv_b, ln_w, ln_b, layer, batch, seq):
    tm = ROW_TILE
    nt = seq // tm
    c = u.shape[1]
    ratio = tm // HALO
    params, param_specs = _conv_params(conv_w, conv_b, ln_w, ln_b, layer)
    return pl.pallas_call(
        functools.partial(_conv_kernel, tm=tm),
        grid=(batch, nt),
        in_specs=[
            pl.BlockSpec((tm, c), lambda b, t: (b * nt + t, 0)),
            pl.BlockSpec((HALO, c), lambda b, t: (jnp.maximum((b * nt + t) * ratio - 1, 0), 0)),
            _resident(u_side.shape),
        ] + param_specs,
        out_specs=pl.BlockSpec((tm, c), lambda b, t: (b * nt + t, 0)),
        out_shape=jax.ShapeDtypeStruct((batch * seq, c), BF16),
        scratch_shapes=[pltpu.VMEM((HALO + tm, c), F32),
                        pltpu.VMEM((SUBLANES - 1, HALO + tm - SUBLANES, c), F32)],
        compiler_params=pltpu.CompilerParams(
            dimension_semantics=("arbitrary", "arbitrary"), vmem_limit_bytes=VMEM_LIMIT),
        name="conv",
    )(u, u, u_side, *params)


def _side_conv_kernel(us_ref, st_ref, w_ref, b_ref, lnw_ref, lnb_ref, z_ref, ns_ref, *, n_dec):
    c = us_ref.shape[1]
    hist_len = CONV_K - 1
    full = jnp.concatenate([jnp.zeros((hist_len, c), F32), us_ref[0:N_META, :]], axis=0)
    acc = full[0:N_META] * w_ref[0:1, :]
    for k in range(1, CONV_K):
        acc = acc + full[k:k + N_META] * w_ref[k:k + 1, :]
    z_ref[0:N_META, :] = _ln_swish(acc + b_ref[...], lnw_ref[...], lnb_ref[...])
    for n in range(n_dec):
        u_n = us_ref[N_META + n:N_META + n + 1, :]
        y = (jnp.sum(st_ref[n] * w_ref[0:hist_len, :], axis=0, keepdims=True)
             + u_n * w_ref[hist_len:CONV_K, :])
        z_ref[N_META + n:N_META + n + 1, :] = _ln_swish(y + b_ref[...], lnw_ref[...], lnb_ref[...])
        ns_ref[n, 0:hist_len - 1, :] = st_ref[n, 1:hist_len, :]
        ns_ref[n, hist_len - 1:hist_len, :] = u_n
    n_pad = z_ref.shape[0] - N_META - n_dec
    z_ref[N_META + n_dec:, :] = jnp.zeros((n_pad, c), F32)


def _side_conv(u_side, state_conv, conv_w, conv_b, ln_w, ln_b, layer):
    rows, c = u_side.shape
    n_dec = state_conv.shape[1]
    params, param_specs = _conv_params(conv_w, conv_b, ln_w, ln_b, layer)
    return pl.pallas_call(
        functools.partial(_side_conv_kernel, n_dec=n_dec),
        grid=(1,),
        in_specs=[_resident(u_side.shape), _resident_layer(state_conv, layer)] + param_specs,
        out_specs=[pl.BlockSpec((rows, c), lambda i: (0, 0)),
                   pl.BlockSpec(state_conv.shape[1:], lambda i: (0, 0, 0))],
        out_shape=[jax.ShapeDtypeStruct((rows, c), F32),
                   jax.ShapeDtypeStruct(state_conv.shape[1:], F32)],
        name="side_conv",
    )(u_side, state_conv, *params)


def _out_kernel(z_ref, cg_ref, o_ref, ag_ref, h_ref, wpw_ref, sub_ref, wout_ref, *rest,
                post_scale, final_norm):
    conv_y = _dot(z_ref[...].astype(BF16), wpw_ref[...]) * _silu(cg_ref[...])
    sub = sub_ref[...]
    atts = []
    for h in range(N_HEADS):
        c0 = h * HEAD_DV
        att = (_rms(o_ref[:, c0:c0 + HEAD_DV], sub, 1e-5) * post_scale) * _silu(ag_ref[:, c0:c0 + HEAD_DV])
        atts.append(att.astype(BF16))
    mixed = jnp.concatenate([conv_y.astype(BF16)] + atts, axis=1)
    h_new = h_ref[...] + _dot(mixed, wout_ref[...])
    if final_norm:
        fw_ref, y_ref = rest
        y_ref[...] = _rms(h_new, fw_ref[...], 1e-6)
    else:
        (y_ref,) = rest
        y_ref[...] = h_new


def _out_proj(z, cg, o, ag, h, w_pw2_bf, subln_w, w_out_bf, layer, lam_init, tm, final_norm_w=None):
    rows, d = h.shape
    c_conv = z.shape[1]
    d_att = o.shape[1]
    blk = lambda w: pl.BlockSpec((tm, w), lambda i: (i, 0))
    sub_rows = subln_w.reshape(subln_w.shape[0], 1, HEAD_DV)
    in_specs = [blk(c_conv), blk(c_conv), blk(d_att), blk(d_att), blk(d),
                _resident_layer(w_pw2_bf, layer), _resident_layer(sub_rows, layer),
                _resident_layer(w_out_bf, layer)]
    args = [z, cg, o, ag, h, w_pw2_bf, sub_rows, w_out_bf]
    if final_norm_w is not None:
        in_specs.append(_resident((1, d)))
        args.append(final_norm_w.reshape(1, d))
    return pl.pallas_call(
        functools.partial(_out_kernel, post_scale=1.0 - lam_init,
                          final_norm=final_norm_w is not None),
        grid=(rows // tm,),
        in_specs=in_specs,
        out_specs=blk(d),
        out_shape=jax.ShapeDtypeStruct((rows, d), F32),
        compiler_params=pltpu.CompilerParams(
            dimension_semantics=("arbitrary",), vmem_limit_bytes=VMEM_LIMIT),
        name="out_proj",
    )(*args)


def _kt_kernel(*refs, depth):
    k_refs, km_refs, o_ref = refs[:depth], refs[depth:2 * depth], refs[2 * depth]
    layer = pl.program_id(0)
    for j in range(depth):
        @pl.when(layer == j)
        def _(j=j):
            o_ref[:, 0:N_META] = km_refs[j][0:N_META, :].T
            o_ref[:, N_META:] = k_refs[j][...].T


def _key_output(k_layers, k_side_layers, batch, seq):
    depth = len(k_layers)
    d_att = k_layers[0].shape[1]
    n_slab = d_att // KT_SLAB
    side_rows = k_side_layers[0].shape[0]

    def when_layer(j, cur, last):
        return lambda l, b, s: tuple(
            jnp.where(l < j, 0, jnp.where(l > j, e, c)) for c, e in zip(cur(b, s), last))

    k_specs = [pl.BlockSpec((seq, KT_SLAB), when_layer(j, lambda b, s: (b, s), (batch - 1, n_slab - 1)))
               for j in range(depth)]
    km_specs = [pl.BlockSpec((side_rows, KT_SLAB), when_layer(j, lambda b, s: (0, s), (0, n_slab - 1)))
                for j in range(depth)]
    return pl.pallas_call(
        functools.partial(_kt_kernel, depth=depth),
        grid=(depth, batch, n_slab),
        in_specs=k_specs + km_specs,
        out_specs=pl.BlockSpec((None, None, KT_SLAB, N_META + seq), lambda l, b, s: (l, b, s, 0)),
        out_shape=jax.ShapeDtypeStruct((depth, batch, d_att, N_META + seq), F32),
        compiler_params=pltpu.CompilerParams(
            dimension_semantics=("arbitrary", "arbitrary", "arbitrary"),
            vmem_limit_bytes=VMEM_LIMIT),
        name="key_output",
    )(*k_layers, *k_side_layers)


def _value_copies(v_refs, vs_refs, o_ref, sems, *, batch, rows_meta, rows_seq):
    copies = []
    for l, (v_ref, vs_ref) in enumerate(zip(v_refs, vs_refs)):
        for b in range(batch):
            n = len(copies)
            copies.append(pltpu.make_async_copy(
                vs_ref.at[pl.ds(0, rows_meta), :], o_ref.at[l, b, pl.ds(0, rows_meta), :], sems.at[n]))
            copies.append(pltpu.make_async_copy(
                v_ref.at[pl.ds(b * rows_seq, rows_seq), :],
                o_ref.at[l, b, pl.ds(rows_meta, rows_seq), :], sems.at[n + 1]))
    return copies


def _value_output_kernel(*refs, depth, **dims):
    copies = _value_copies(refs[:depth], refs[depth:2 * depth], refs[2 * depth], refs[2 * depth + 1],
                           **dims)
    for cp in copies:
        cp.start()
    for cp in copies:
        cp.wait()


def _value_output(v_layers, v_side_layers, batch, seq):
    depth = len(v_layers)
    rows_meta, rows_seq = N_META * N_HEADS, seq * N_HEADS
    any_spec = pl.BlockSpec(memory_space=pl.ANY)
    return pl.pallas_call(
        functools.partial(_value_output_kernel, depth=depth, batch=batch,
                          rows_meta=rows_meta, rows_seq=rows_seq),
        in_specs=[any_spec] * (2 * depth),
        out_specs=any_spec,
        out_shape=jax.ShapeDtypeStruct((depth, batch, rows_meta + rows_seq, HEAD_DV), F32),
        scratch_shapes=[pltpu.SemaphoreType.DMA((2 * depth * batch,))],
        name="value_output",
    )(*v_layers, *v_side_layers)


def _bias_by_distance(rel_bias, n):
    d = jnp.arange(n)
    max_exact = N_BUCKETS // 2
    nf = jnp.maximum(d, 1).astype(F32)
    large = max_exact + (jnp.log(nf / max_exact) / math.log(MAX_DIST / max_exact)
                         * (N_BUCKETS - max_exact)).astype(jnp.int32)
    bucket = jnp.where(d < max_exact, d, jnp.minimum(large, N_BUCKETS - 1))
    t = rel_bias[bucket] - rel_bias[N_BUCKETS - 1]
    return t.T.astype(F32)


def _bias_block(t, offset, n_keys, n_queries):
    heads = t.shape[0]
    n_diag = n_keys + n_queries - 1
    lo = offset - (n_keys - 1)
    n_neg = max(0, -lo)
    diag = jnp.concatenate([jnp.full((heads, n_neg), NEG, F32),
                            t[:, lo + n_neg:lo + n_diag]], axis=1)
    period = jnp.concatenate([diag, jnp.zeros((heads, 1), F32)], axis=1)
    skew = jnp.tile(period, (1, n_keys))[:, :n_keys * n_diag].reshape(heads, n_keys, n_diag)
    return skew[:, :, n_keys - 1:n_keys - 1 + n_queries]


def _meta_bias(t, tile):
    near = _bias_block(t, N_META, N_META, tile)
    far = jnp.zeros_like(near)
    masked = jnp.full((near.shape[0], META_PAD - N_META, tile), NEG, F32)
    return jnp.stack([jnp.concatenate([near, masked], axis=1),
                      jnp.concatenate([far, masked], axis=1)], axis=1)


def kernel(x_prompt, x_sample, cache_k, cache_v, state_conv, page_table, meta_tokens, rel_bias,
           norm_w, w_in, conv_w, conv_b, conv_ln_w, conv_ln_b, w_pw2, lambda_q1, lambda_k1,
           lambda_q2, lambda_k2, subln_w, w_out, final_norm_w):
    batch, seq, d = x_prompt.shape
    depth = w_in.shape[0]
    db = x_sample.shape[0]
    c_conv = d // 2
    d_att = d - c_conv
    n_pool, page_size = cache_k.shape[1], cache_k.shape[2]
    tile = ATT_TILE
    rows2 = 2 * N_HEADS

    cache_kt = jnp.transpose(cache_k, (0, 1, 3, 4, 5, 2)).reshape(depth, n_pool, d_att, page_size)
    cache_vr = cache_v.reshape(depth, n_pool, page_size * N_HEADS, HEAD_DV)

    t = _bias_by_distance(rel_bias, tile + N_META)
    bnear = jnp.stack([_bias_block(t, 0, BIAS_BLOCK, BIAS_BLOCK),
                       _bias_block(t, BIAS_BLOCK, BIAS_BLOCK, BIAS_BLOCK)], axis=1)
    bmeta = _meta_bias(t, tile)
    bias_meta = jnp.swapaxes(_bias_block(t, 0, N_META, N_META), 1, 2)
    t_rows = jnp.repeat(t, 2, axis=0)
    blast = t_rows[:, page_size:0:-1]
    bself = jnp.where(jnp.arange(rows2)[None, :] == 0, t_rows[:, 0:1], NEG)

    lam_inits = [0.8 - 0.6 * math.exp(-0.3 * l) for l in range(depth)]
    f32 = lambda a: a.astype(F32)
    lam = (jnp.exp(jnp.sum(f32(lambda_q1) * f32(lambda_k1), axis=-1))
           - jnp.exp(jnp.sum(f32(lambda_q2) * f32(lambda_k2), axis=-1))
           + jnp.asarray(lam_inits, F32))

    hp = x_prompt.reshape(batch * seq, d)
    side = jnp.concatenate([meta_tokens.astype(F32), x_sample.reshape(db, d),
                            jnp.zeros((SIDE_ROWS - N_META - db, d), F32)], axis=0)
    w_in_bf = w_in.astype(BF16)
    w_pw2_bf = w_pw2.astype(BF16)
    w_out_bf = w_out.astype(BF16)
    conv_p = (conv_w, conv_b, conv_ln_w, conv_ln_b)
    k_l, ksd_l, v_l, vsd_l, cp_l, ks_l, vs_l, cs_l = [], [], [], [], [], [], [], []
    for l in range(depth):
        fin = final_norm_w if l == depth - 1 else None

        us, cgs, qs, ks, vs, ags, kbs, vbs = _inproj(side, norm_w, w_in_bf, l, SIDE_ROWS)
        z_side, new_state = _side_conv(us, state_conv, *conv_p, l)
        o_meta = _meta_attn(qs, kbs, vbs, bias_meta, lam, l)
        o_dec = _decode_attn(page_table, lam, qs, kbs, vbs, cache_kt, cache_vr, l, blast, bself)
        o_side = jnp.concatenate([o_meta, o_dec[:, 0],
                                  jnp.zeros((SIDE_ROWS - N_META - db, d_att), F32)], axis=0)

        u, cg, q, k, v, ag, kb, vb = _inproj(hp, norm_w, w_in_bf, l, ROW_TILE)
        z = _conv(u, us, *conv_p, l, batch, seq)
        o = _flash(q, kb, vb, kbs, vbs, bmeta, bnear, lam, l, batch, seq)
        hp = _out_proj(z, cg, o, ag, hp, w_pw2_bf, subln_w, w_out_bf, l, lam_inits[l], OUT_TILE, fin)
        side = _out_proj(z_side, cgs, o_side, ags, side, w_pw2_bf, subln_w, w_out_bf, l,
                         lam_inits[l], SIDE_ROWS, fin)

        k_l.append(k)
        ksd_l.append(ks)
        v_l.append(v)
        vsd_l.append(vs)
        cp_l.append(u.reshape(batch, seq, c_conv)[:, seq - (CONV_K - 1):])
        ks_l.append(ks[N_META:N_META + db])
        vs_l.append(vs.reshape(SIDE_ROWS, N_HEADS, HEAD_DV)[N_META:N_META + db])
        cs_l.append(new_state)

    lp = seq + N_META
    kt = _key_output(k_l, ksd_l, batch, seq)
    new_k = jnp.transpose(kt.reshape(depth, batch, N_HEADS, 2, HEAD_DK, lp), (0, 1, 5, 2, 3, 4))
    new_v = _value_output(v_l, vsd_l, batch, seq)
    return (hp.reshape(batch, seq, d),
            side[N_META:N_META + db].reshape(db, 1, d),
            new_k,
            new_v.reshape(depth, batch, lp, N_HEADS, HEAD_DV),
            jnp.stack(cp_l),
            jnp.stack(ks_l).reshape(depth, db, 1, N_HEADS, 2, HEAD_DK),
            jnp.stack(vs_l).reshape(depth, db, 1, N_HEADS, HEAD_DV),
            jnp.stack(cs_l))
```

```python
import functools
import math

import jax
import jax.numpy as jnp
from jax import lax
from jax.experimental import pallas as pl
from jax.experimental.pallas import tpu as pltpu

F32 = jnp.float32
BF16 = jnp.bfloat16

N_META = 16
N_HEADS = 8
HEAD_DV = 128
HEAD_DK = 64
CONV_K = 31
N_BUCKETS = 32
MAX_DIST = 128
NEG = -1e30
SUBLANES = 8
BF16_ROWS = 16
SIDE_ROWS = 32
HALO = 32
ROW_TILE = 256
OUT_TILE = 512
ATT_TILE = 512
ATT_HEADS_PER_STEP = 2
META_PAD = 128
BIAS_BLOCK = MAX_DIST
CONV_CHUNK = 32
PAGES_PER_STEP = 8
KT_SLAB = 256
VMEM_LIMIT = 56 * 1024 * 1024


def _dot(a, b):
    return jnp.dot(a, b, preferred_element_type=F32)


def _dot_nt(a, b):
    return lax.dot_general(a, b, (((1,), (1,)), ((), ())), preferred_element_type=F32)


def _silu(x):
    return x * jax.nn.sigmoid(x)


def _rms(x, w, eps):
    ms = jnp.mean(x * x, axis=-1, keepdims=True)
    return (x * lax.rsqrt(ms + eps)) * w


def _resident(shape):
    return pl.BlockSpec(shape, lambda *_: (0,) * len(shape), pipeline_mode=pl.Buffered(1))


def _resident_layer(stacked, layer):
    shape = stacked.shape[1:]
    return pl.BlockSpec((None,) + shape, lambda *_: (layer,) + (0,) * len(shape),
                        pipeline_mode=pl.Buffered(1))


def _smem():
    return pl.BlockSpec(memory_space=pltpu.SMEM)


def _inproj_kernel(x_ref, nw_ref, w_ref, u_ref, cg_ref, q_ref, k_ref, v_ref, ag_ref,
                   kb_ref, vb_ref, *, c_conv, d_att, scale):
    xb = _rms(x_ref[...], nw_ref[...], 1e-6).astype(BF16)

    def mm(c0, width):
        return _dot(xb, w_ref[:, c0:c0 + width])

    a = mm(0, c_conv)
    g = mm(c_conv, c_conv)
    u_ref[...] = a * jax.nn.sigmoid(g)
    cg_ref[...] = mm(2 * c_conv, c_conv)
    base = 3 * c_conv
    q_ref[...] = (mm(base, d_att) * scale).astype(BF16)
    k = mm(base + d_att, d_att)
    k_ref[...] = k
    kb_ref[...] = k.astype(BF16)
    v = mm(base + 2 * d_att, d_att)
    tm = v.shape[0]
    for h in range(N_HEADS):
        v_ref[pl.ds(h, tm, stride=N_HEADS), :] = v[:, h * HEAD_DV:(h + 1) * HEAD_DV]
    vb_ref[...] = v.astype(BF16)
    ag_ref[...] = mm(base + 3 * d_att, d_att)


def _inproj(x, norm_w, w_in_bf, layer, tm):
    rows, d = x.shape
    c_conv = d // 2
    d_att = d - c_conv
    row_blk = lambda w: pl.BlockSpec((tm, w), lambda i: (i, 0))
    shapes = [
        jax.ShapeDtypeStruct((rows, c_conv), F32),
        jax.ShapeDtypeStruct((rows, c_conv), F32),
        jax.ShapeDtypeStruct((rows, d_att), BF16),
        jax.ShapeDtypeStruct((rows, d_att), F32),
        jax.ShapeDtypeStruct((rows * N_HEADS, HEAD_DV), F32),
        jax.ShapeDtypeStruct((rows, d_att), F32),
        jax.ShapeDtypeStruct((rows, d_att), BF16),
        jax.ShapeDtypeStruct((rows, d_att), BF16),
    ]
    norm_rows = norm_w.reshape(norm_w.shape[0], 1, d)
    return pl.pallas_call(
        functools.partial(_inproj_kernel, c_conv=c_conv, d_att=d_att, scale=HEAD_DK ** -0.5),
        grid=(rows // tm,),
        in_specs=[row_blk(d), _resident_layer(norm_rows, layer), _resident_layer(w_in_bf, layer)],
        out_specs=[pl.BlockSpec((tm * s.shape[0] // rows, s.shape[1]), lambda i: (i, 0))
                   for s in shapes],
        out_shape=shapes,
        compiler_params=pltpu.CompilerParams(
            dimension_semantics=("arbitrary",), vmem_limit_bytes=VMEM_LIMIT),
        name="inproj",
    )(x, norm_rows, w_in_bf)


def _flash_kernel(lam_ref, q_ref, k_ref, v_ref, km_ref, vm_ref, bmeta_ref, bnear_ref,
                  o_ref, st_sc, vxt_sc, *, tile, n_q, layer):
    qi = pl.program_id(2)
    heads = st_sc.shape[0]
    cols = lambda hh: slice(hh * HEAD_DV, (hh + 1) * HEAD_DV)
    pad = jnp.zeros((META_PAD - N_META, HEAD_DV), BF16)
    both = lambda b: jnp.concatenate([b, b], axis=1)

    q2t, k_meta = [], []
    for hh in range(heads):
        qt = q_ref[:, cols(hh)].T
        row = lax.broadcasted_iota(jnp.int32, qt.shape, 0)
        zero = jnp.zeros_like(qt)
        q2t.append(jnp.concatenate([jnp.where(row < HEAD_DK, qt, zero),
                                    jnp.where(row >= HEAD_DK, qt, zero)], axis=1))
        k_meta.append(jnp.concatenate([km_ref[:, cols(hh)], pad], axis=0))

    @pl.when(qi == 0)
    def _():
        for hh in range(heads):
            v_meta = jnp.concatenate([vm_ref[:, cols(hh)], pad], axis=0)
            vxt_sc[hh, 0:HEAD_DV, 0:META_PAD] = v_meta.T
            vxt_sc[hh, 0:HEAD_DV, META_PAD:] = v_ref[:, cols(hh)].T
            vxt_sc[hh, HEAD_DV:, :] = jnp.ones((BF16_ROWS, vxt_sc.shape[2]), BF16)

    def near_bias(hh, offset):
        nb = tile // BIAS_BLOCK
        neg = jnp.full((BIAS_BLOCK, BIAS_BLOCK), NEG, F32)
        zero = jnp.zeros((BIAS_BLOCK, BIAS_BLOCK), F32)
        rows = []
        for bj in range(nb):
            blocks = []
            for bi in range(nb):
                lead = offset // BIAS_BLOCK + bi - bj
                blocks.append(neg if lead < 0 else bnear_ref[hh, lead] if lead < 2 else zero)
            rows.append(jnp.concatenate(blocks, axis=1))
        return jnp.concatenate(rows, axis=0)

    def variant(c):
        m = [None] * heads
        acc = [None] * heads

        def pass1(hh, ki):
            if ki < 0:
                s = _dot(k_meta[hh], q2t[hh]) + both(bmeta_ref[hh, 0 if c == 0 else 1])
                st_sc[hh, 0:META_PAD, :] = s
                m[hh] = jnp.max(s, axis=0, keepdims=True)
                return
            s = _dot(k_ref[ki * tile:(ki + 1) * tile, cols(hh)], q2t[hh])
            if ki == c:
                s = s + both(near_bias(hh, 0))
            elif ki == c - 1:
                s = s + both(near_bias(hh, tile))
            lo = META_PAD + ki * tile
            st_sc[hh, lo:lo + tile, :] = s
            m[hh] = jnp.maximum(m[hh], jnp.max(s, axis=0, keepdims=True))

        def pass2(hh, ki):
            lo, width = (0, META_PAD) if ki < 0 else (META_PAD + ki * tile, tile)
            p = jnp.exp(st_sc[hh, lo:lo + width, :] - m[hh]).astype(BF16)
            d = _dot(vxt_sc[hh, :, lo:lo + width], p)
            acc[hh] = d if acc[hh] is None else acc[hh] + d

        for stage in range(heads + 1):
            for ki in range(-1, c + 1):
                if stage < heads:
                    pass1(stage, ki)
                if stage >= 1:
                    pass2(stage - 1, ki)
        for hh in range(heads):
            on = acc[hh][0:HEAD_DV] / acc[hh][HEAD_DV:HEAD_DV + 1]
            o_ref[:, cols(hh)] = (on[:, :tile] - lam_ref[layer] * on[:, tile:]).T

    for c in range(n_q):
        pl.when(qi == c)(functools.partial(variant, c))


def _flash(q, kb, vb, kb_side, vb_side, bmeta, bnear, lam, layer, batch, seq):
    tile = ATT_TILE
    nq = seq // tile
    d_att = q.shape[1]
    hps = ATT_HEADS_PER_STEP
    width = hps * HEAD_DV
    meta = pl.BlockSpec((N_META, width), lambda b, h, qi: (0, h))
    per_head = lambda *dims: pl.BlockSpec((hps,) + dims, lambda b, h, qi: (h,) + (0,) * len(dims))
    return pl.pallas_call(
        functools.partial(_flash_kernel, tile=tile, n_q=nq, layer=layer),
        grid=(batch, N_HEADS // hps, nq),
        in_specs=[
            _smem(),
            pl.BlockSpec((tile, width), lambda b, h, qi: (b * nq + qi, h)),
            pl.BlockSpec((seq, width), lambda b, h, qi: (b, h)),
            pl.BlockSpec((seq, width), lambda b, h, qi: (b, h)),
            meta, meta,
            per_head(2, META_PAD, tile), per_head(2, BIAS_BLOCK, BIAS_BLOCK),
        ],
        out_specs=pl.BlockSpec((tile, width), lambda b, h, qi: (b * nq + qi, h)),
        out_shape=jax.ShapeDtypeStruct((batch * seq, d_att), F32),
        scratch_shapes=[pltpu.VMEM((hps, META_PAD + seq, 2 * tile), F32),
                        pltpu.VMEM((hps, HEAD_DV + BF16_ROWS, META_PAD + seq), BF16)],
        compiler_params=pltpu.CompilerParams(
            dimension_semantics=("arbitrary", "arbitrary", "arbitrary"),
            vmem_limit_bytes=VMEM_LIMIT),
        name="prompt_attention",
    )(lam, q, kb, vb, kb_side, vb_side, bmeta, bnear)


def _meta_attn_kernel(lam_ref, q_ref, k_ref, v_ref, bias_ref, o_ref, *, layer):
    lam = lam_ref[layer]
    lane = lax.broadcasted_iota(jnp.int32, (q_ref.shape[0], HEAD_DV), 1)
    for h in range(N_HEADS):
        c0 = h * HEAD_DV
        bias = bias_ref[h]
        q = q_ref[:, c0:c0 + HEAD_DV]
        k = k_ref[:, c0:c0 + HEAD_DV]
        zero = jnp.zeros_like(q)
        ps = []
        for qc in (jnp.where(lane < HEAD_DK, q, zero), jnp.where(lane >= HEAD_DK, q, zero)):
            s = _dot_nt(qc, k) + bias
            e = jnp.exp(s - jnp.max(s, axis=-1, keepdims=True))
            ps.append(e / jnp.sum(e, axis=-1, keepdims=True))
        wgt = (ps[0] - lam * ps[1]).astype(BF16)
        o_ref[:, c0:c0 + HEAD_DV] = _dot(wgt, v_ref[:, c0:c0 + HEAD_DV])


def _meta_attn(q_side, kb_side, vb_side, bias, lam, layer):
    d_att = q_side.shape[1]
    top = pl.BlockSpec((N_META, d_att), lambda i: (0, 0))
    return pl.pallas_call(
        functools.partial(_meta_attn_kernel, layer=layer),
        grid=(1,),
        in_specs=[_smem(), top, top, top, pl.BlockSpec(bias.shape, lambda i: (0, 0, 0))],
        out_specs=top,
        out_shape=jax.ShapeDtypeStruct((N_META, d_att), F32),
        name="meta_attention",
    )(lam, q_side, kb_side, vb_side, bias)


def _decode_kernel(pt_ref, lam_ref, q_ref, *rest, pages, n_steps, layer):
    del pt_ref
    kp = rest[:pages]
    vp = rest[pages:2 * pages]
    kself_ref, vself_ref, blast_ref, bself_ref, o_ref, m_sc, l_sc, acc_sc = rest[2 * pages:]
    seq_id = pl.program_id(0)
    step_id = pl.program_id(1)
    rows = 2 * N_HEADS
    d_att = q_ref.shape[-1]
    row = lax.broadcasted_iota(jnp.int32, (rows, d_att), 0)
    lane = lax.broadcasted_iota(jnp.int32, (rows, d_att), 1)

    def own_row(ref):
        return jnp.sum(jnp.where(row == seq_id, ref[...].astype(F32), 0.0), axis=0, keepdims=True)

    def as_first_row(x):
        return jnp.where(row == 0, jnp.broadcast_to(x, (rows, d_att)), 0.0).astype(BF16)

    seg = lax.shift_right_logical(lane, int(math.log2(HEAD_DK)))
    qbd = jnp.where(seg == row, jnp.broadcast_to(own_row(q_ref), (rows, d_att)), 0.0).astype(BF16)

    @pl.when(step_id == 0)
    def _():
        m_sc[...] = jnp.full(m_sc.shape, NEG, F32)
        l_sc[...] = jnp.zeros(l_sc.shape, F32)
        acc_sc[...] = jnp.zeros(acc_sc.shape, F32)

    def update(s_parts, v_parts):
        m_prev = m_sc[...]
        m_new = m_prev
        for s in s_parts:
            m_new = jnp.maximum(m_new, jnp.max(s, axis=-1, keepdims=True))
        alpha = jnp.exp(m_prev - m_new)
        l_new = alpha * l_sc[...]
        acc = alpha * acc_sc[...]
        for s, v in zip(s_parts, v_parts):
            p = jnp.exp(s - m_new)
            l_new = l_new + jnp.sum(p, axis=-1, keepdims=True)
            acc = acc + _dot(p.astype(BF16), v)
        m_sc[...] = m_new
        l_sc[...] = l_new
        acc_sc[...] = acc

    is_last = step_id == n_steps - 1
    page_size = blast_ref.shape[1]
    s_parts = []
    v_parts = []
    for g in range(pages):
        s = _dot(qbd, kp[g][...].astype(BF16))
        if g == pages - 1:
            s = s + jnp.where(is_last, blast_ref[...], 0.0)
        s_parts.append(s)
        v_heads = [vp[g][pl.ds(h, page_size, stride=N_HEADS), :] for h in range(N_HEADS)]
        v_parts.append(jnp.concatenate(v_heads, axis=1).astype(BF16))
    update(s_parts, v_parts)

    @pl.when(is_last)
    def _():
        k_self = as_first_row(own_row(kself_ref))
        v_self = as_first_row(own_row(vself_ref))
        update([_dot_nt(qbd, k_self) + bself_ref[...]], [v_self])
        on = acc_sc[...] / l_sc[...]
        own = lax.shift_right_logical(lane, int(math.log2(HEAD_DV))) == lax.shift_right_logical(row, 1)
        comp0 = (row & 1) == 0
        sel = jnp.where(own, jnp.where(comp0, 1.0, -lam_ref[layer]), 0.0)
        o_ref[...] = jnp.sum(on * sel, axis=0, keepdims=True)


def _decode_attn(page_table, lam, q_side, kb_side, vb_side, cache_kt, cache_vr, layer, blast, bself):
    db, n_pages = page_table.shape
    d_att, page_size = cache_kt.shape[2], cache_kt.shape[3]
    pages = PAGES_PER_STEP
    n_steps = n_pages // pages
    rows = 2 * N_HEADS
    assert rows == BF16_ROWS and db <= rows and N_META == rows

    def page_spec(arr):
        return lambda g: pl.BlockSpec(
            (None, None) + arr.shape[2:], lambda b, s, pt: (layer, pt[b, s * pages + g], 0, 0))

    k_spec, v_spec = page_spec(cache_kt), page_spec(cache_vr)
    dec_rows = pl.BlockSpec((rows, d_att), lambda b, s, pt: (N_META // rows, 0))
    whole = lambda shape: pl.BlockSpec(shape, lambda b, s, pt: (0,) * len(shape))
    grid_spec = pltpu.PrefetchScalarGridSpec(
        num_scalar_prefetch=1,
        grid=(db, n_steps),
        in_specs=([_smem(), dec_rows]
                  + [k_spec(g) for g in range(pages)]
                  + [v_spec(g) for g in range(pages)]
                  + [dec_rows, dec_rows, whole((rows, page_size)), whole((rows, rows))]),
        out_specs=pl.BlockSpec((None, 1, d_att), lambda b, s, pt: (b, 0, 0)),
        scratch_shapes=[pltpu.VMEM((rows, 1), F32), pltpu.VMEM((rows, 1), F32),
                        pltpu.VMEM((rows, d_att), F32)],
    )
    return pl.pallas_call(
        functools.partial(_decode_kernel, pages=pages, n_steps=n_steps, layer=layer),
        grid_spec=grid_spec,
        out_shape=jax.ShapeDtypeStruct((db, 1, d_att), F32),
        compiler_params=pltpu.CompilerParams(
            dimension_semantics=("arbitrary", "arbitrary"), vmem_limit_bytes=VMEM_LIMIT),
        name="decode_attention",
    )(page_table, lam, q_side, *([cache_kt] * pages), *([cache_vr] * pages),
      kb_side, vb_side, blast, bself)


def _ln_swish(y, lnw, lnb):
    mu = jnp.mean(y, axis=-1, keepdims=True)
    yc = y - mu
    var = jnp.mean(yc * yc, axis=-1, keepdims=True)
    return _silu((yc * lax.rsqrt(var + 1e-5)) * lnw + lnb)


def _conv_kernel(u_ref, halo_ref, us_ref, w_ref, b_ref, lnw_ref, lnb_ref, z_ref, full_sc,
                 shift_sc, *, tm):
    ti = pl.program_id(1)
    c = u_ref.shape[1]
    prefix = jnp.concatenate([jnp.zeros((HALO - N_META, c), F32), us_ref[0:N_META, :]], axis=0)
    full_sc[0:HALO, :] = jnp.where(ti == 0, prefix, halo_ref[...])
    full_sc[HALO:HALO + tm, :] = u_ref[...]
    span = shift_sc.shape[1]
    for r in range(1, SUBLANES):
        shift_sc[r - 1] = full_sc[r:r + span, :]
    first = HALO - (CONV_K - 1)

    for r0 in range(0, tm, CONV_CHUNK):
        acc = None
        for k in range(CONV_K):
            r = (first + k) % SUBLANES
            base = r0 + first + k - r
            if r == 0:
                window = full_sc[base:base + CONV_CHUNK, :]
            else:
                window = shift_sc[r - 1, base:base + CONV_CHUNK, :]
            term = window * w_ref[k:k + 1, :]
            acc = term if acc is None else acc + term
        z = _ln_swish(acc + b_ref[...], lnw_ref[...], lnb_ref[...])
        z_ref[r0:r0 + CONV_CHUNK, :] = z.astype(BF16)


def _conv_params(conv_w, conv_b, ln_w, ln_b, layer):
    depth, c = conv_b.shape
    vec = lambda a: a.reshape(depth, 1, c)
    arrays = (conv_w, vec(conv_b), vec(ln_w), vec(ln_b))
    return arrays, [_resident_layer(a, layer) for a in arrays]


def _conv(u, u_side, conv_w, conv_b, ln_w, ln_b, layer, batch, seq):
    tm = ROW_TILE
    nt = seq // tm
    c = u.shape[1]
    ratio = tm // HALO
    params, param_specs = _conv_params(conv_w, conv_b, ln_w, ln_b, layer)
    return pl.pallas_call(
        functools.partial(_conv_kernel, tm=tm),
        grid=(batch, nt),
        in_specs=[
            pl.BlockSpec((tm, c), lambda b, t: (b * nt + t, 0)),
            pl.BlockSpec((HALO, c), lambda b, t: (jnp.maximum((b * nt + t) * ratio - 1, 0), 0)),
            _resident(u_side.shape),
        ] + param_specs,
        out_specs=pl.BlockSpec((tm, c), lambda b, t: (b * nt + t, 0)),
        out_shape=jax.ShapeDtypeStruct((batch * seq, c), BF16),
        scratch_shapes=[pltpu.VMEM((HALO + tm, c), F32),
                        pltpu.VMEM((SUBLANES - 1, HALO + tm - SUBLANES, c), F32)],
        compiler_params=pltpu.CompilerParams(
            dimension_semantics=("arbitrary", "arbitrary"), vmem_limit_bytes=VMEM_LIMIT),
        name="conv",
    )(u, u, u_side, *params)


def _side_conv_kernel(us_ref, st_ref, w_ref, b_ref, lnw_ref, lnb_ref, z_ref, ns_ref, *, n_dec):
    c = us_ref.shape[1]
    hist_len = CONV_K - 1
    full = jnp.concatenate([jnp.zeros((hist_len, c), F32), us_ref[0:N_META, :]], axis=0)
    acc = full[0:N_META] * w_ref[0:1, :]
    for k in range(1, CONV_K):
        acc = acc + full[k:k + N_META] * w_ref[k:k + 1, :]
    z_ref[0:N_META, :] = _ln_swish(acc + b_ref[...], lnw_ref[...], lnb_ref[...])
    for n in range(n_dec):
        u_n = us_ref[N_META + n:N_META + n + 1, :]
        y = (jnp.sum(st_ref[n] * w_ref[0:hist_len, :], axis=0, keepdims=True)
             + u_n * w_ref[hist_len:CONV_K, :])
        z_ref[N_META + n:N_META + n + 1, :] = _ln_swish(y + b_ref[...], lnw_ref[...], lnb_ref[...])
        ns_ref[n, 0:hist_len - 1, :] = st_ref[n, 1:hist_len, :]
        ns_ref[n, hist_len - 1:hist_len, :] = u_n
    n_pad = z_ref.shape[0] - N_META - n_dec
    z_ref[N_META + n_dec:, :] = jnp.zeros((n_pad, c), F32)


def _side_conv(u_side, state_conv, conv_w, conv_b, ln_w, ln_b, layer):
    rows, c = u_side.shape
    n_dec = state_conv.shape[1]
    params, param_specs = _conv_params(conv_w, conv_b, ln_w, ln_b, layer)
    return pl.pallas_call(
        functools.partial(_side_conv_kernel, n_dec=n_dec),
        grid=(1,),
        in_specs=[_resident(u_side.shape), _resident_layer(state_conv, layer)] + param_specs,
        out_specs=[pl.BlockSpec((rows, c), lambda i: (0, 0)),
                   pl.BlockSpec(state_conv.shape[1:], lambda i: (0, 0, 0))],
        out_shape=[jax.ShapeDtypeStruct((rows, c), F32),
                   jax.ShapeDtypeStruct(state_conv.shape[1:], F32)],
        name="side_conv",
    )(u_side, state_conv, *params)


def _out_kernel(z_ref, cg_ref, o_ref, ag_ref, h_ref, wpw_ref, sub_ref, wout_ref, *rest,
                post_scale, final_norm):
    conv_y = _dot(z_ref[...].astype(BF16), wpw_ref[...]) * _silu(cg_ref[...])
    sub = sub_ref[...]
    atts = []
    for h in range(N_HEADS):
        c0 = h * HEAD_DV
        att = (_rms(o_ref[:, c0:c0 + HEAD_DV], sub, 1e-5) * post_scale) * _silu(ag_ref[:, c0:c0 + HEAD_DV])
        atts.append(att.astype(BF16))
    mixed = jnp.concatenate([conv_y.astype(BF16)] + atts, axis=1)
    h_new = h_ref[...] + _dot(mixed, wout_ref[...])
    if final_norm:
        fw_ref, y_ref = rest
        y_ref[...] = _rms(h_new, fw_ref[...], 1e-6)
    else:
        (y_ref,) = rest
        y_ref[...] = h_new


def _out_proj(z, cg, o, ag, h, w_pw2_bf, subln_w, w_out_bf, layer, lam_init, tm, final_norm_w=None):
    rows, d = h.shape
    c_conv = z.shape[1]
    d_att = o.shape[1]
    blk = lambda w: pl.BlockSpec((tm, w), lambda i: (i, 0))
    sub_rows = subln_w.reshape(subln_w.shape[0], 1, HEAD_DV)
    in_specs = [blk(c_conv), blk(c_conv), blk(d_att), blk(d_att), blk(d),
                _resident_layer(w_pw2_bf, layer), _resident_layer(sub_rows, layer),
                _resident_layer(w_out_bf, layer)]
    args = [z, cg, o, ag, h, w_pw2_bf, sub_rows, w_out_bf]
    if final_norm_w is not None:
        in_specs.append(_resident((1, d)))
        args.append(final_norm_w.reshape(1, d))
    return pl.pallas_call(
        functools.partial(_out_kernel, post_scale=1.0 - lam_init,
                          final_norm=final_norm_w is not None),
        grid=(rows // tm,),
        in_specs=in_specs,
        out_specs=blk(d),
        out_shape=jax.ShapeDtypeStruct((rows, d), F32),
        compiler_params=pltpu.CompilerParams(
            dimension_semantics=("arbitrary",), vmem_limit_bytes=VMEM_LIMIT),
        name="out_proj",
    )(*args)


def _when_layer(j, cur, last):
    return lambda l, *rest: tuple(
        jnp.where(l < j, 0, jnp.where(l > j, e, c)) for c, e in zip(cur(*rest), last))


def _kt_kernel(*refs, depth):
    k_refs, km_refs, o_ref = refs[:depth], refs[depth:2 * depth], refs[2 * depth]
    layer = pl.program_id(0)
    for j in range(depth):
        @pl.when(layer == j)
        def _(j=j):
            o_ref[:, 0:N_META] = km_refs[j][0:N_META, :].T
            o_ref[:, N_META:] = k_refs[j][...].T


def _key_output(k_layers, k_side_layers, batch, seq):
    depth = len(k_layers)
    d_att = k_layers[0].shape[1]
    n_slab = d_att // KT_SLAB
    side_rows = k_side_layers[0].shape[0]

    k_specs = [pl.BlockSpec((seq, KT_SLAB), _when_layer(j, lambda b, s: (b, s), (batch - 1, n_slab - 1)))
               for j in range(depth)]
    km_specs = [pl.BlockSpec((side_rows, KT_SLAB), _when_layer(j, lambda b, s: (0, s), (0, n_slab - 1)))
                for j in range(depth)]
    return pl.pallas_call(
        functools.partial(_kt_kernel, depth=depth),
        grid=(depth, batch, n_slab),
        in_specs=k_specs + km_specs,
        out_specs=pl.BlockSpec((None, None, KT_SLAB, N_META + seq), lambda l, b, s: (l, b, s, 0)),
        out_shape=jax.ShapeDtypeStruct((depth, batch, d_att, N_META + seq), F32),
        compiler_params=pltpu.CompilerParams(
            dimension_semantics=("arbitrary", "arbitrary", "arbitrary"),
            vmem_limit_bytes=VMEM_LIMIT),
        name="key_output",
    )(*k_layers, *k_side_layers)


def _value_output_kernel(*refs, depth, rows_meta):
    v_refs, vs_refs, o_ref = refs[:depth], refs[depth:2 * depth], refs[2 * depth]
    layer = pl.program_id(0)
    for j in range(depth):
        @pl.when(layer == j)
        def _(j=j):
            o_ref[0:rows_meta, :] = vs_refs[j][0:rows_meta, :]
            o_ref[rows_meta:, :] = v_refs[j][...]


def _value_output(v_layers, v_side_layers, batch, seq):
    depth = len(v_layers)
    rows_meta, rows_seq = N_META * N_HEADS, seq * N_HEADS
    v_specs = [pl.BlockSpec((rows_seq, HEAD_DV), _when_layer(j, lambda b: (b, 0), (batch - 1, 0)),
                            pipeline_mode=pl.Buffered(1)) for j in range(depth)]
    vs_specs = [pl.BlockSpec(v_side_layers[0].shape, lambda l, b: (0, 0)) for _ in range(depth)]
    return pl.pallas_call(
        functools.partial(_value_output_kernel, depth=depth, rows_meta=rows_meta),
        grid=(depth, batch),
        in_specs=v_specs + vs_specs,
        out_specs=pl.BlockSpec((None, None, rows_meta + rows_seq, HEAD_DV), lambda l, b: (l, b, 0, 0)),
        out_shape=jax.ShapeDtypeStruct((depth, batch, rows_meta + rows_seq, HEAD_DV), F32),
        compiler_params=pltpu.CompilerParams(
            dimension_semantics=("arbitrary", "arbitrary"), vmem_limit_bytes=VMEM_LIMIT),
        name="value_output",
    )(*v_layers, *v_side_layers)


def _bias_by_distance(rel_bias, n):
    d = jnp.arange(n)
    max_exact = N_BUCKETS // 2
    nf = jnp.maximum(d, 1).astype(F32)
    large = max_exact + (jnp.log(nf / max_exact) / math.log(MAX_DIST / max_exact)
                         * (N_BUCKETS - max_exact)).astype(jnp.int32)
    bucket = jnp.where(d < max_exact, d, jnp.minimum(large, N_BUCKETS - 1))
    t = rel_bias[bucket] - rel_bias[N_BUCKETS - 1]
    return t.T.astype(F32)


def _bias_block(t, offset, n_keys, n_queries):
    heads = t.shape[0]
    n_diag = n_keys + n_queries - 1
    lo = offset - (n_keys - 1)
    n_neg = max(0, -lo)
    diag = jnp.concatenate([jnp.full((heads, n_neg), NEG, F32),
                            t[:, lo + n_neg:lo + n_diag]], axis=1)
    period = jnp.concatenate([diag, jnp.zeros((heads, 1), F32)], axis=1)
    skew = jnp.tile(period, (1, n_keys))[:, :n_keys * n_diag].reshape(heads, n_keys, n_diag)
    return skew[:, :, n_keys - 1:n_keys - 1 + n_queries]


def _meta_bias(t, tile):
    near = _bias_block(t, N_META, N_META, tile)
    far = jnp.zeros_like(near)
    masked = jnp.full((near.shape[0], META_PAD - N_META, tile), NEG, F32)
    return jnp.stack([jnp.concatenate([near, masked], axis=1),
                      jnp.concatenate([far, masked], axis=1)], axis=1)


def kernel(x_prompt, x_sample, cache_k, cache_v, state_conv, page_table, meta_tokens, rel_bias,
           norm_w, w_in, conv_w, conv_b, conv_ln_w, conv_ln_b, w_pw2, lambda_q1, lambda_k1,
           lambda_q2, lambda_k2, subln_w, w_out, final_norm_w):
    batch, seq, d = x_prompt.shape
    depth = w_in.shape[0]
    db = x_sample.shape[0]
    c_conv = d // 2
    d_att = d - c_conv
    n_pool, page_size = cache_k.shape[1], cache_k.shape[2]
    tile = ATT_TILE
    rows2 = 2 * N_HEADS

    cache_kt = jnp.transpose(cache_k, (0, 1, 3, 4, 5, 2)).reshape(depth, n_pool, d_att, page_size)
    cache_vr = cache_v.reshape(depth, n_pool, page_size * N_HEADS, HEAD_DV)

    t = _bias_by_distance(rel_bias, tile + N_META)
    bnear = jnp.stack([_bias_block(t, 0, BIAS_BLOCK, BIAS_BLOCK),
                       _bias_block(t, BIAS_BLOCK, BIAS_BLOCK, BIAS_BLOCK)], axis=1)
    bmeta = _meta_bias(t, tile)
    bias_meta = jnp.swapaxes(_bias_block(t, 0, N_META, N_META), 1, 2)
    t_rows = jnp.repeat(t, 2, axis=0)
    blast = t_rows[:, page_size:0:-1]
    bself = jnp.where(jnp.arange(rows2)[None, :] == 0, t_rows[:, 0:1], NEG)

    lam_inits = [0.8 - 0.6 * math.exp(-0.3 * l) for l in range(depth)]
    f32 = lambda a: a.astype(F32)
    lam = (jnp.exp(jnp.sum(f32(lambda_q1) * f32(lambda_k1), axis=-1))
           - jnp.exp(jnp.sum(f32(lambda_q2) * f32(lambda_k2), axis=-1))
           + jnp.asarray(lam_inits, F32))

    hp = x_prompt.reshape(batch * seq, d)
    side = jnp.concatenate([meta_tokens.astype(F32), x_sample.reshape(db, d),
                            jnp.zeros((SIDE_ROWS - N_META - db, d), F32)], axis=0)
    w_in_bf = w_in.astype(BF16)
    w_pw2_bf = w_pw2.astype(BF16)
    w_out_bf = w_out.astype(BF16)
    conv_p = (conv_w, conv_b, conv_ln_w, conv_ln_b)
    k_l, ksd_l, v_l, vsd_l, cp_l, ks_l, vs_l, cs_l = [], [], [], [], [], [], [], []
    for l in range(depth):
        fin = final_norm_w if l == depth - 1 else None

        us, cgs, qs, ks, vs, ags, kbs, vbs = _inproj(side, norm_w, w_in_bf, l, SIDE_ROWS)
        z_side, new_state = _side_conv(us, state_conv, *conv_p, l)
        o_meta = _meta_attn(qs, kbs, vbs, bias_meta, lam, l)
        o_dec = _decode_attn(page_table, lam, qs, kbs, vbs, cache_kt, cache_vr, l, blast, bself)
        o_side = jnp.concatenate([o_meta, o_dec[:, 0],
                                  jnp.zeros((SIDE_ROWS - N_META - db, d_att), F32)], axis=0)

        u, cg, q, k, v, ag, kb, vb = _inproj(hp, norm_w, w_in_bf, l, ROW_TILE)
        z = _conv(u, us, *conv_p, l, batch, seq)
        o = _flash(q, kb, vb, kbs, vbs, bmeta, bnear, lam, l, batch, seq)
        hp = _out_proj(z, cg, o, ag, hp, w_pw2_bf, subln_w, w_out_bf, l, lam_inits[l], OUT_TILE, fin)
        side = _out_proj(z_side, cgs, o_side, ags, side, w_pw2_bf, subln_w, w_out_bf, l,
                         lam_inits[l], SIDE_ROWS, fin)

        k_l.append(k)
        ksd_l.append(ks)
        v_l.append(v)
        vsd_l.append(vs)
        cp_l.append(u.reshape(batch, seq, c_conv)[:, seq - (CONV_K - 1):])
        ks_l.append(ks[N_META:N_META + db])
        vs_l.append(vs.reshape(SIDE_ROWS, N_HEADS, HEAD_DV)[N_META:N_META + db])
        cs_l.append(new_state)

    lp = seq + N_META
    kt = _key_output(k_l, ksd_l, batch, seq)
    new_k = jnp.transpose(kt.reshape(depth, batch, N_HEADS, 2, HEAD_DK, lp), (0, 1, 5, 2, 3, 4))
    new_v = _value_output(v_l, vsd_l, batch, seq)
    return (hp.reshape(batch, seq, d),
            side[N_META:N_META + db].reshape(db, 1, d),
            new_k,
            new_v.reshape(depth, batch, lp, N_HEADS, HEAD_DV),
            jnp.stack(cp_l),
            jnp.stack(ks_l).reshape(depth, db, 1, N_HEADS, 2, HEAD_DK),
            jnp.stack(vs_l).reshape(depth, db, 1, N_HEADS, HEAD_DV),
            jnp.stack(cs_l))
```

```python
import functools
import math

import jax
import jax.numpy as jnp
from jax import lax
from jax.experimental import pallas as pl
from jax.experimental.pallas import tpu as pltpu

F32 = jnp.float32
BF16 = jnp.bfloat16

N_META = 16
N_HEADS = 8
HEAD_DV = 128
HEAD_DK = 64
CONV_K = 31
N_BUCKETS = 32
MAX_DIST = 128
NEG = -1e30
SUBLANES = 8
BF16_ROWS = 16
SIDE_ROWS = 32
HALO = 32
ROW_TILE = 256
OUT_TILE = 512
ATT_TILE = 512
ATT_HEADS_PER_STEP = 2
META_PAD = 128
BIAS_BLOCK = MAX_DIST
CONV_CHUNK = 32
PAGES_PER_STEP = 8
KT_SLAB = 512
VMEM_LIMIT = 56 * 1024 * 1024


def _dot(a, b):
    return jnp.dot(a, b, preferred_element_type=F32)


def _dot_nt(a, b):
    return lax.dot_general(a, b, (((1,), (1,)), ((), ())), preferred_element_type=F32)


def _silu(x):
    return x * jax.nn.sigmoid(x)


def _rms(x, w, eps):
    ms = jnp.mean(x * x, axis=-1, keepdims=True)
    return (x * lax.rsqrt(ms + eps)) * w


def _resident(shape):
    return pl.BlockSpec(shape, lambda *_: (0,) * len(shape), pipeline_mode=pl.Buffered(1))


def _resident_layer(stacked, layer):
    shape = stacked.shape[1:]
    return pl.BlockSpec((None,) + shape, lambda *_: (layer,) + (0,) * len(shape),
                        pipeline_mode=pl.Buffered(1))


def _smem():
    return pl.BlockSpec(memory_space=pltpu.SMEM)


def _inproj_kernel(x_ref, nw_ref, w_ref, u_ref, cg_ref, q_ref, k_ref, v_ref, ag_ref,
                   kb_ref, vb_ref, *, c_conv, d_att, scale):
    xb = _rms(x_ref[...], nw_ref[...], 1e-6).astype(BF16)

    def mm(c0, width):
        return _dot(xb, w_ref[:, c0:c0 + width])

    a = mm(0, c_conv)
    g = mm(c_conv, c_conv)
    u_ref[...] = a * jax.nn.sigmoid(g)
    cg_ref[...] = mm(2 * c_conv, c_conv)
    base = 3 * c_conv
    q_ref[...] = (mm(base, d_att) * scale).astype(BF16)
    k = mm(base + d_att, d_att)
    k_ref[...] = k
    kb_ref[...] = k.astype(BF16)
    v = mm(base + 2 * d_att, d_att)
    tm = v.shape[0]
    for h in range(N_HEADS):
        v_ref[pl.ds(h, tm, stride=N_HEADS), :] = v[:, h * HEAD_DV:(h + 1) * HEAD_DV]
    vb_ref[...] = v.astype(BF16)
    ag_ref[...] = mm(base + 3 * d_att, d_att)


def _inproj(x, norm_w, w_in_bf, layer, tm):
    rows, d = x.shape
    c_conv = d // 2
    d_att = d - c_conv
    row_blk = lambda w: pl.BlockSpec((tm, w), lambda i: (i, 0))
    shapes = [
        jax.ShapeDtypeStruct((rows, c_conv), F32),
        jax.ShapeDtypeStruct((rows, c_conv), F32),
        jax.ShapeDtypeStruct((rows, d_att), BF16),
        jax.ShapeDtypeStruct((rows, d_att), F32),
        jax.ShapeDtypeStruct((rows * N_HEADS, HEAD_DV), F32),
        jax.ShapeDtypeStruct((rows, d_att), F32),
        jax.ShapeDtypeStruct((rows, d_att), BF16),
        jax.ShapeDtypeStruct((rows, d_att), BF16),
    ]
    norm_rows = norm_w.reshape(norm_w.shape[0], 1, d)
    return pl.pallas_call(
        functools.partial(_inproj_kernel, c_conv=c_conv, d_att=d_att, scale=HEAD_DK ** -0.5),
        grid=(rows // tm,),
        in_specs=[row_blk(d), _resident_layer(norm_rows, layer), _resident_layer(w_in_bf, layer)],
        out_specs=[pl.BlockSpec((tm * s.shape[0] // rows, s.shape[1]), lambda i: (i, 0))
                   for s in shapes],
        out_shape=shapes,
        compiler_params=pltpu.CompilerParams(
            dimension_semantics=("arbitrary",), vmem_limit_bytes=VMEM_LIMIT),
        name="inproj",
    )(x, norm_rows, w_in_bf)


def _flash_kernel(lam_ref, q_ref, k_ref, v_ref, km_ref, vm_ref, bmeta_ref, bnear_ref,
                  o_ref, st_sc, vxt_sc, *, tile, n_q, layer):
    qi = pl.program_id(2)
    heads = st_sc.shape[0]
    cols = lambda hh: slice(hh * HEAD_DV, (hh + 1) * HEAD_DV)
    pad = jnp.zeros((META_PAD - N_META, HEAD_DV), BF16)
    both = lambda b: jnp.concatenate([b, b], axis=1)

    q2t, k_meta = [], []
    for hh in range(heads):
        qt = q_ref[:, cols(hh)].T
        row = lax.broadcasted_iota(jnp.int32, qt.shape, 0)
        zero = jnp.zeros_like(qt)
        q2t.append(jnp.concatenate([jnp.where(row < HEAD_DK, qt, zero),
                                    jnp.where(row >= HEAD_DK, qt, zero)], axis=1))
        k_meta.append(jnp.concatenate([km_ref[:, cols(hh)], pad], axis=0))

    @pl.when(qi == 0)
    def _():
        for hh in range(heads):
            v_meta = jnp.concatenate([vm_ref[:, cols(hh)], pad], axis=0)
            vxt_sc[hh, 0:HEAD_DV, 0:META_PAD] = v_meta.T
            vxt_sc[hh, 0:HEAD_DV, META_PAD:] = v_ref[:, cols(hh)].T
            vxt_sc[hh, HEAD_DV:, :] = jnp.ones((BF16_ROWS, vxt_sc.shape[2]), BF16)

    def near_bias(hh, offset):
        nb = tile // BIAS_BLOCK
        neg = jnp.full((BIAS_BLOCK, BIAS_BLOCK), NEG, F32)
        zero = jnp.zeros((BIAS_BLOCK, BIAS_BLOCK), F32)
        rows = []
        for bj in range(nb):
            blocks = []
            for bi in range(nb):
                lead = offset // BIAS_BLOCK + bi - bj
                blocks.append(neg if lead < 0 else bnear_ref[hh, lead] if lead < 2 else zero)
            rows.append(jnp.concatenate(blocks, axis=1))
        return jnp.concatenate(rows, axis=0)

    def variant(c):
        m = [None] * heads
        acc = [None] * heads

        def pass1(hh, ki):
            if ki < 0:
                s = _dot(k_meta[hh], q2t[hh]) + both(bmeta_ref[hh, 0 if c == 0 else 1])
                st_sc[hh, 0:META_PAD, :] = s
                m[hh] = jnp.max(s, axis=0, keepdims=True)
                return
            s = _dot(k_ref[ki * tile:(ki + 1) * tile, cols(hh)], q2t[hh])
            if ki == c:
                s = s + both(near_bias(hh, 0))
            elif ki == c - 1:
                s = s + both(near_bias(hh, tile))
            lo = META_PAD + ki * tile
            st_sc[hh, lo:lo + tile, :] = s
            m[hh] = jnp.maximum(m[hh], jnp.max(s, axis=0, keepdims=True))

        def pass2(hh, ki):
            lo, width = (0, META_PAD) if ki < 0 else (META_PAD + ki * tile, tile)
            p = jnp.exp(st_sc[hh, lo:lo + width, :] - m[hh]).astype(BF16)
            d = _dot(vxt_sc[hh, :, lo:lo + width], p)
            acc[hh] = d if acc[hh] is None else acc[hh] + d

        for stage in range(heads + 1):
            for ki in range(-1, c + 1):
                if stage < heads:
                    pass1(stage, ki)
                if stage >= 1:
                    pass2(stage - 1, ki)
        for hh in range(heads):
            on = acc[hh][0:HEAD_DV] / acc[hh][HEAD_DV:HEAD_DV + 1]
            o_ref[:, cols(hh)] = (on[:, :tile] - lam_ref[layer] * on[:, tile:]).T

    for c in range(n_q):
        pl.when(qi == c)(functools.partial(variant, c))


def _flash(q, kb, vb, kb_side, vb_side, bmeta, bnear, lam, layer, batch, seq):
    tile = ATT_TILE
    nq = seq // tile
    d_att = q.shape[1]
    hps = ATT_HEADS_PER_STEP
    width = hps * HEAD_DV
    meta = pl.BlockSpec((N_META, width), lambda b, h, qi: (0, h))
    per_head = lambda *dims: pl.BlockSpec((hps,) + dims, lambda b, h, qi: (h,) + (0,) * len(dims))
    return pl.pallas_call(
        functools.partial(_flash_kernel, tile=tile, n_q=nq, layer=layer),
        grid=(batch, N_HEADS // hps, nq),
        in_specs=[
            _smem(),
            pl.BlockSpec((tile, width), lambda b, h, qi: (b * nq + qi, h)),
            pl.BlockSpec((seq, width), lambda b, h, qi: (b, h)),
            pl.BlockSpec((seq, width), lambda b, h, qi: (b, h)),
            meta, meta,
            per_head(2, META_PAD, tile), per_head(2, BIAS_BLOCK, BIAS_BLOCK),
        ],
        out_specs=pl.BlockSpec((tile, width), lambda b, h, qi: (b * nq + qi, h)),
        out_shape=jax.ShapeDtypeStruct((batch * seq, d_att), F32),
        scratch_shapes=[pltpu.VMEM((hps, META_PAD + seq, 2 * tile), F32),
                        pltpu.VMEM((hps, HEAD_DV + BF16_ROWS, META_PAD + seq), BF16)],
        compiler_params=pltpu.CompilerParams(
            dimension_semantics=("arbitrary", "arbitrary", "arbitrary"),
            vmem_limit_bytes=VMEM_LIMIT),
        name="prompt_attention",
    )(lam, q, kb, vb, kb_side, vb_side, bmeta, bnear)


def _meta_attn_kernel(lam_ref, q_ref, k_ref, v_ref, bias_ref, o_ref, *, layer):
    lam = lam_ref[layer]
    lane = lax.broadcasted_iota(jnp.int32, (q_ref.shape[0], HEAD_DV), 1)
    for h in range(N_HEADS):
        c0 = h * HEAD_DV
        bias = bias_ref[h]
        q = q_ref[:, c0:c0 + HEAD_DV]
        k = k_ref[:, c0:c0 + HEAD_DV]
        zero = jnp.zeros_like(q)
        ps = []
        for qc in (jnp.where(lane < HEAD_DK, q, zero), jnp.where(lane >= HEAD_DK, q, zero)):
            s = _dot_nt(qc, k) + bias
            e = jnp.exp(s - jnp.max(s, axis=-1, keepdims=True))
            ps.append(e / jnp.sum(e, axis=-1, keepdims=True))
        wgt = (ps[0] - lam * ps[1]).astype(BF16)
        o_ref[:, c0:c0 + HEAD_DV] = _dot(wgt, v_ref[:, c0:c0 + HEAD_DV])


def _meta_attn(q_side, kb_side, vb_side, bias, lam, layer):
    d_att = q_side.shape[1]
    top = pl.BlockSpec((N_META, d_att), lambda i: (0, 0))
    return pl.pallas_call(
        functools.partial(_meta_attn_kernel, layer=layer),
        grid=(1,),
        in_specs=[_smem(), top, top, top, pl.BlockSpec(bias.shape, lambda i: (0, 0, 0))],
        out_specs=top,
        out_shape=jax.ShapeDtypeStruct((N_META, d_att), F32),
        name="meta_attention",
    )(lam, q_side, kb_side, vb_side, bias)


def _decode_kernel(pt_ref, lam_ref, q_ref, *rest, pages, n_steps, layer):
    del pt_ref
    kp = rest[:pages]
    vp = rest[pages:2 * pages]
    kself_ref, vself_ref, blast_ref, bself_ref, o_ref, m_sc, l_sc, acc_sc = rest[2 * pages:]
    seq_id = pl.program_id(0)
    step_id = pl.program_id(1)
    rows = 2 * N_HEADS
    d_att = q_ref.shape[-1]
    row = lax.broadcasted_iota(jnp.int32, (rows, d_att), 0)
    lane = lax.broadcasted_iota(jnp.int32, (rows, d_att), 1)

    def own_row(ref):
        return jnp.sum(jnp.where(row == seq_id, ref[...].astype(F32), 0.0), axis=0, keepdims=True)

    def as_first_row(x):
        return jnp.where(row == 0, jnp.broadcast_to(x, (rows, d_att)), 0.0).astype(BF16)

    seg = lax.shift_right_logical(lane, int(math.log2(HEAD_DK)))
    qbd = jnp.where(seg == row, jnp.broadcast_to(own_row(q_ref), (rows, d_att)), 0.0).astype(BF16)

    @pl.when(step_id == 0)
    def _():
        m_sc[...] = jnp.full(m_sc.shape, NEG, F32)
        l_sc[...] = jnp.zeros(l_sc.shape, F32)
        acc_sc[...] = jnp.zeros(acc_sc.shape, F32)

    def update(s_parts, v_parts):
        m_prev = m_sc[...]
        m_new = m_prev
        for s in s_parts:
            m_new = jnp.maximum(m_new, jnp.max(s, axis=-1, keepdims=True))
        alpha = jnp.exp(m_prev - m_new)
        l_new = alpha * l_sc[...]
        acc = alpha * acc_sc[...]
        for s, v in zip(s_parts, v_parts):
            p = jnp.exp(s - m_new)
            l_new = l_new + jnp.sum(p, axis=-1, keepdims=True)
            acc = acc + _dot(p.astype(BF16), v)
        m_sc[...] = m_new
        l_sc[...] = l_new
        acc_sc[...] = acc

    is_last = step_id == n_steps - 1
    page_size = blast_ref.shape[1]
    s_parts = []
    v_parts = []
    for g in range(pages):
        s = _dot(qbd, kp[g][...].astype(BF16))
        if g == pages - 1:
            s = s + jnp.where(is_last, blast_ref[...], 0.0)
        s_parts.append(s)
        v_heads = [vp[g][pl.ds(h, page_size, stride=N_HEADS), :] for h in range(N_HEADS)]
        v_parts.append(jnp.concatenate(v_heads, axis=1).astype(BF16))
    update(s_parts, v_parts)

    @pl.when(is_last)
    def _():
        k_self = as_first_row(own_row(kself_ref))
        v_self = as_first_row(own_row(vself_ref))
        update([_dot_nt(qbd, k_self) + bself_ref[...]], [v_self])
        on = acc_sc[...] / l_sc[...]
        own = lax.shift_right_logical(lane, int(math.log2(HEAD_DV))) == lax.shift_right_logical(row, 1)
        comp0 = (row & 1) == 0
        sel = jnp.where(own, jnp.where(comp0, 1.0, -lam_ref[layer]), 0.0)
        o_ref[...] = jnp.sum(on * sel, axis=0, keepdims=True)


def _decode_attn(page_table, lam, q_side, kb_side, vb_side, cache_kt, cache_vr, layer, blast, bself):
    db, n_pages = page_table.shape
    d_att, page_size = cache_kt.shape[2], cache_kt.shape[3]
    pages = PAGES_PER_STEP
    n_steps = n_pages // pages
    rows = 2 * N_HEADS
    assert rows == BF16_ROWS and db <= rows and N_META == rows

    def page_spec(arr):
        return lambda g: pl.BlockSpec(
            (None, None) + arr.shape[2:], lambda b, s, pt: (layer, pt[b, s * pages + g], 0, 0))

    k_spec, v_spec = page_spec(cache_kt), page_spec(cache_vr)
    dec_rows = pl.BlockSpec((rows, d_att), lambda b, s, pt: (N_META // rows, 0))
    whole = lambda shape: pl.BlockSpec(shape, lambda b, s, pt: (0,) * len(shape))
    grid_spec = pltpu.PrefetchScalarGridSpec(
        num_scalar_prefetch=1,
        grid=(db, n_steps),
        in_specs=([_smem(), dec_rows]
                  + [k_spec(g) for g in range(pages)]
                  + [v_spec(g) for g in range(pages)]
                  + [dec_rows, dec_rows, whole((rows, page_size)), whole((rows, rows))]),
        out_specs=pl.BlockSpec((None, 1, d_att), lambda b, s, pt: (b, 0, 0)),
        scratch_shapes=[pltpu.VMEM((rows, 1), F32), pltpu.VMEM((rows, 1), F32),
                        pltpu.VMEM((rows, d_att), F32)],
    )
    return pl.pallas_call(
        functools.partial(_decode_kernel, pages=pages, n_steps=n_steps, layer=layer),
        grid_spec=grid_spec,
        out_shape=jax.ShapeDtypeStruct((db, 1, d_att), F32),
        compiler_params=pltpu.CompilerParams(
            dimension_semantics=("arbitrary", "arbitrary"), vmem_limit_bytes=VMEM_LIMIT),
        name="decode_attention",
    )(page_table, lam, q_side, *([cache_kt] * pages), *([cache_vr] * pages),
      kb_side, vb_side, blast, bself)


def _ln_swish(y, lnw, lnb):
    mu = jnp.mean(y, axis=-1, keepdims=True)
    yc = y - mu
    var = jnp.mean(yc * yc, axis=-1, keepdims=True)
    return _silu((yc * lax.rsqrt(var + 1e-5)) * lnw + lnb)


def _conv_kernel(u_ref, halo_ref, us_ref, w_ref, b_ref, lnw_ref, lnb_ref, z_ref, full_sc,
                 shift_sc, *, tm):
    ti = pl.program_id(1)
    c = u_ref.shape[1]
    prefix = jnp.concatenate([jnp.zeros((HALO - N_META, c), F32), us_ref[0:N_META, :]], axis=0)
    full_sc[0:HALO, :] = jnp.where(ti == 0, prefix, halo_ref[...])
    full_sc[HALO:HALO + tm, :] = u_ref[...]
    span = shift_sc.shape[1]
    for r in range(1, SUBLANES):
        shift_sc[r - 1] = full_sc[r:r + span, :]
    first = HALO - (CONV_K - 1)

    for r0 in range(0, tm, CONV_CHUNK):
        acc = None
        for k in range(CONV_K):
            r = (first + k) % SUBLANES
            base = r0 + first + k - r
            if r == 0:
                window = full_sc[base:base + CONV_CHUNK, :]
            else:
                window = shift_sc[r - 1, base:base + CONV_CHUNK, :]
            term = window * w_ref[k:k + 1, :]
            acc = term if acc is None else acc + term
        z = _ln_swish(acc + b_ref[...], lnw_ref[...], lnb_ref[...])
        z_ref[r0:r0 + CONV_CHUNK, :] = z.astype(BF16)


def _conv_params(conv_w, conv_b, ln_w, ln_b, layer):
    depth, c = conv_b.shape
    vec = lambda a: a.reshape(depth, 1, c)
    arrays = (conv_w, vec(conv_b), vec(ln_w), vec(ln_b))
    return arrays, [_resident_layer(a, layer) for a in arrays]


def _conv(u, u_side, conv_w, conv_b, ln_w, ln_b, layer, batch, seq):
    tm = ROW_TILE
    nt = seq // tm
    c = u.shape[1]
    ratio = tm // HALO
    params, param_specs = _conv_params(conv_w, conv_b, ln_w, ln_b, layer)
    return pl.pallas_call(
        functools.partial(_conv_kernel, tm=tm),
        grid=(batch, nt),
        in_specs=[
            pl.BlockSpec((tm, c), lambda b, t: (b * nt + t, 0)),
            pl.BlockSpec((HALO, c), lambda b, t: (jnp.maximum((b * nt + t) * ratio - 1, 0), 0)),
            _resident(u_side.shape),
        ] + param_specs,
        out_specs=pl.BlockSpec((tm, c), lambda b, t: (b * nt + t, 0)),
        out_shape=jax.ShapeDtypeStruct((batch * seq, c), BF16),
        scratch_shapes=[pltpu.VMEM((HALO + tm, c), F32),
                        pltpu.VMEM((SUBLANES - 1, HALO + tm - SUBLANES, c), F32)],
        compiler_params=pltpu.CompilerParams(
            dimension_semantics=("arbitrary", "arbitrary"), vmem_limit_bytes=VMEM_LIMIT),
        name="conv",
    )(u, u, u_side, *params)


def _side_conv_kernel(us_ref, st_ref, w_ref, b_ref, lnw_ref, lnb_ref, z_ref, ns_ref, *, n_dec):
    c = us_ref.shape[1]
    hist_len = CONV_K - 1
    full = jnp.concatenate([jnp.zeros((hist_len, c), F32), us_ref[0:N_META, :]], axis=0)
    acc = full[0:N_META] * w_ref[0:1, :]
    for k in range(1, CONV_K):
        acc = acc + full[k:k + N_META] * w_ref[k:k + 1, :]
    z_ref[0:N_META, :] = _ln_swish(acc + b_ref[...], lnw_ref[...], lnb_ref[...])
    for n in range(n_dec):
        u_n = us_ref[N_META + n:N_META + n + 1, :]
        y = (jnp.sum(st_ref[n] * w_ref[0:hist_len, :], axis=0, keepdims=True)
             + u_n * w_ref[hist_len:CONV_K, :])
        z_ref[N_META + n:N_META + n + 1, :] = _ln_swish(y + b_ref[...], lnw_ref[...], lnb_ref[...])
        ns_ref[n, 0:hist_len - 1, :] = st_ref[n, 1:hist_len, :]
        ns_ref[n, hist_len - 1:hist_len, :] = u_n
    n_pad = z_ref.shape[0] - N_META - n_dec
    z_ref[N_META + n_dec:, :] = jnp.zeros((n_pad, c), F32)


def _side_conv(u_side, state_conv, conv_w, conv_b, ln_w, ln_b, layer):
    rows, c = u_side.shape
    n_dec = state_conv.shape[1]
    params, param_specs = _conv_params(conv_w, conv_b, ln_w, ln_b, layer)
    return pl.pallas_call(
        functools.partial(_side_conv_kernel, n_dec=n_dec),
        grid=(1,),
        in_specs=[_resident(u_side.shape), _resident_layer(state_conv, layer)] + param_specs,
        out_specs=[pl.BlockSpec((rows, c), lambda i: (0, 0)),
                   pl.BlockSpec(state_conv.shape[1:], lambda i: (0, 0, 0))],
        out_shape=[jax.ShapeDtypeStruct((rows, c), F32),
                   jax.ShapeDtypeStruct(state_conv.shape[1:], F32)],
        name="side_conv",
    )(u_side, state_conv, *params)


def _out_kernel(z_ref, cg_ref, o_ref, ag_ref, h_ref, wpw_ref, sub_ref, wout_ref, *rest,
                post_scale, final_norm):
    conv_y = _dot(z_ref[...].astype(BF16), wpw_ref[...]) * _silu(cg_ref[...])
    sub = sub_ref[...]
    atts = []
    for h in range(N_HEADS):
        c0 = h * HEAD_DV
        att = (_rms(o_ref[:, c0:c0 + HEAD_DV], sub, 1e-5) * post_scale) * _silu(ag_ref[:, c0:c0 + HEAD_DV])
        atts.append(att.astype(BF16))
    mixed = jnp.concatenate([conv_y.astype(BF16)] + atts, axis=1)
    h_new = h_ref[...] + _dot(mixed, wout_ref[...])
    if final_norm:
        fw_ref, y_ref = rest
        y_ref[...] = _rms(h_new, fw_ref[...], 1e-6)
    else:
        (y_ref,) = rest
        y_ref[...] = h_new


def _out_proj(z, cg, o, ag, h, w_pw2_bf, subln_w, w_out_bf, layer, lam_init, tm, final_norm_w=None):
    rows, d = h.shape
    c_conv = z.shape[1]
    d_att = o.shape[1]
    blk = lambda w: pl.BlockSpec((tm, w), lambda i: (i, 0))
    sub_rows = subln_w.reshape(subln_w.shape[0], 1, HEAD_DV)
    in_specs = [blk(c_conv), blk(c_conv), blk(d_att), blk(d_att), blk(d),
                _resident_layer(w_pw2_bf, layer), _resident_layer(sub_rows, layer),
                _resident_layer(w_out_bf, layer)]
    args = [z, cg, o, ag, h, w_pw2_bf, sub_rows, w_out_bf]
    if final_norm_w is not None:
        in_specs.append(_resident((1, d)))
        args.append(final_norm_w.reshape(1, d))
    return pl.pallas_call(
        functools.partial(_out_kernel, post_scale=1.0 - lam_init,
                          final_norm=final_norm_w is not None),
        grid=(rows // tm,),
        in_specs=in_specs,
        out_specs=blk(d),
        out_shape=jax.ShapeDtypeStruct((rows, d), F32),
        compiler_params=pltpu.CompilerParams(
            dimension_semantics=("arbitrary",), vmem_limit_bytes=VMEM_LIMIT),
        name="out_proj",
    )(*args)


def _when_layer(j, cur, last):
    return lambda l, *rest: tuple(
        jnp.where(l < j, 0, jnp.where(l > j, e, c)) for c, e in zip(cur(*rest), last))


def _kt_kernel(*refs, depth):
    k_refs, km_refs, o_ref = refs[:depth], refs[depth:2 * depth], refs[2 * depth]
    layer = pl.program_id(0)
    for j in range(depth):
        @pl.when(layer == j)
        def _(j=j):
            o_ref[:, 0:N_META] = km_refs[j][0:N_META, :].T
            o_ref[:, N_META:] = k_refs[j][...].T


def _key_output(k_layers, k_side_layers, batch, seq):
    depth = len(k_layers)
    d_att = k_layers[0].shape[1]
    n_slab = d_att // KT_SLAB
    side_rows = k_side_layers[0].shape[0]

    k_specs = [pl.BlockSpec((seq, KT_SLAB), _when_layer(j, lambda b, s: (b, s), (batch - 1, n_slab - 1)))
               for j in range(depth)]
    km_specs = [pl.BlockSpec((side_rows, KT_SLAB), _when_layer(j, lambda b, s: (0, s), (0, n_slab - 1)))
                for j in range(depth)]
    return pl.pallas_call(
        functools.partial(_kt_kernel, depth=depth),
        grid=(depth, batch, n_slab),
        in_specs=k_specs + km_specs,
        out_specs=pl.BlockSpec((None, None, KT_SLAB, N_META + seq), lambda l, b, s: (l, b, s, 0)),
        out_shape=jax.ShapeDtypeStruct((depth, batch, d_att, N_META + seq), F32),
        compiler_params=pltpu.CompilerParams(
            dimension_semantics=("arbitrary", "arbitrary", "arbitrary"),
            vmem_limit_bytes=VMEM_LIMIT),
        name="key_output",
    )(*k_layers, *k_side_layers)


def _value_output_kernel(*refs, depth, rows_meta):
    v_refs, vs_refs, o_ref = refs[:depth], refs[depth:2 * depth], refs[2 * depth]
    layer = pl.program_id(0)
    for j in range(depth):
        @pl.when(layer == j)
        def _(j=j):
            o_ref[0:rows_meta, :] = vs_refs[j][0:rows_meta, :]
            o_ref[rows_meta:, :] = v_refs[j][...]


def _value_output(v_layers, v_side_layers, batch, seq):
    depth = len(v_layers)
    rows_meta, rows_seq = N_META * N_HEADS, seq * N_HEADS
    v_specs = [pl.BlockSpec((rows_seq, HEAD_DV), _when_layer(j, lambda b: (b, 0), (batch - 1, 0)),
                            pipeline_mode=pl.Buffered(1)) for j in range(depth)]
    vs_specs = [pl.BlockSpec(v_side_layers[0].shape, lambda l, b: (0, 0)) for _ in range(depth)]
    return pl.pallas_call(
        functools.partial(_value_output_kernel, depth=depth, rows_meta=rows_meta),
        grid=(depth, batch),
        in_specs=v_specs + vs_specs,
        out_specs=pl.BlockSpec((None, None, rows_meta + rows_seq, HEAD_DV), lambda l, b: (l, b, 0, 0)),
        out_shape=jax.ShapeDtypeStruct((depth, batch, rows_meta + rows_seq, HEAD_DV), F32),
        compiler_params=pltpu.CompilerParams(
            dimension_semantics=("arbitrary", "arbitrary"), vmem_limit_bytes=VMEM_LIMIT),
        name="value_output",
    )(*v_layers, *v_side_layers)


def _bias_by_distance(rel_bias, n):
    d = jnp.arange(n)
    max_exact = N_BUCKETS // 2
    nf = jnp.maximum(d, 1).astype(F32)
    large = max_exact + (jnp.log(nf / max_exact) / math.log(MAX_DIST / max_exact)
                         * (N_BUCKETS - max_exact)).astype(jnp.int32)
    bucket = jnp.where(d < max_exact, d, jnp.minimum(large, N_BUCKETS - 1))
    t = rel_bias[bucket] - rel_bias[N_BUCKETS - 1]
    return t.T.astype(F32)


def _bias_block(t, offset, n_keys, n_queries):
    heads = t.shape[0]
    n_diag = n_keys + n_queries - 1
    lo = offset - (n_keys - 1)
    n_neg = max(0, -lo)
    diag = jnp.concatenate([jnp.full((heads, n_neg), NEG, F32),
                            t[:, lo + n_neg:lo + n_diag]], axis=1)
    period = jnp.concatenate([diag, jnp.zeros((heads, 1), F32)], axis=1)
    skew = jnp.tile(period, (1, n_keys))[:, :n_keys * n_diag].reshape(heads, n_keys, n_diag)
    return skew[:, :, n_keys - 1:n_keys - 1 + n_queries]


def _meta_bias(t, tile):
    near = _bias_block(t, N_META, N_META, tile)
    far = jnp.zeros_like(near)
    masked = jnp.full((near.shape[0], META_PAD - N_META, tile), NEG, F32)
    return jnp.stack([jnp.concatenate([near, masked], axis=1),
                      jnp.concatenate([far, masked], axis=1)], axis=1)


def kernel(x_prompt, x_sample, cache_k, cache_v, state_conv, page_table, meta_tokens, rel_bias,
           norm_w, w_in, conv_w, conv_b, conv_ln_w, conv_ln_b, w_pw2, lambda_q1, lambda_k1,
           lambda_q2, lambda_k2, subln_w, w_out, final_norm_w):
    batch, seq, d = x_prompt.shape
    depth = w_in.shape[0]
    db = x_sample.shape[0]
    c_conv = d // 2
    d_att = d - c_conv
    n_pool, page_size = cache_k.shape[1], cache_k.shape[2]
    tile = ATT_TILE
    rows2 = 2 * N_HEADS

    cache_kt = jnp.transpose(cache_k, (0, 1, 3, 4, 5, 2)).reshape(depth, n_pool, d_att, page_size)
    cache_vr = cache_v.reshape(depth, n_pool, page_size * N_HEADS, HEAD_DV)

    t = _bias_by_distance(rel_bias, tile + N_META)
    bnear = jnp.stack([_bias_block(t, 0, BIAS_BLOCK, BIAS_BLOCK),
                       _bias_block(t, BIAS_BLOCK, BIAS_BLOCK, BIAS_BLOCK)], axis=1)
    bmeta = _meta_bias(t, tile)
    bias_meta = jnp.swapaxes(_bias_block(t, 0, N_META, N_META), 1, 2)
    t_rows = jnp.repeat(t, 2, axis=0)
    blast = t_rows[:, page_size:0:-1]
    bself = jnp.where(jnp.arange(rows2)[None, :] == 0, t_rows[:, 0:1], NEG)

    lam_inits = [0.8 - 0.6 * math.exp(-0.3 * l) for l in range(depth)]
    f32 = lambda a: a.astype(F32)
    lam = (jnp.exp(jnp.sum(f32(lambda_q1) * f32(lambda_k1), axis=-1))
           - jnp.exp(jnp.sum(f32(lambda_q2) * f32(lambda_k2), axis=-1))
           + jnp.asarray(lam_inits, F32))

    hp = x_prompt.reshape(batch * seq, d)
    side = jnp.concatenate([meta_tokens.astype(F32), x_sample.reshape(db, d),
                            jnp.zeros((SIDE_ROWS - N_META - db, d), F32)], axis=0)
    w_in_bf = w_in.astype(BF16)
    w_pw2_bf = w_pw2.astype(BF16)
    w_out_bf = w_out.astype(BF16)
    conv_p = (conv_w, conv_b, conv_ln_w, conv_ln_b)
    k_l, ksd_l, v_l, vsd_l, cp_l, ks_l, vs_l, cs_l = [], [], [], [], [], [], [], []
    for l in range(depth):
        fin = final_norm_w if l == depth - 1 else None

        us, cgs, qs, ks, vs, ags, kbs, vbs = _inproj(side, norm_w, w_in_bf, l, SIDE_ROWS)
        z_side, new_state = _side_conv(us, state_conv, *conv_p, l)
        o_meta = _meta_attn(qs, kbs, vbs, bias_meta, lam, l)
        o_dec = _decode_attn(page_table, lam, qs, kbs, vbs, cache_kt, cache_vr, l, blast, bself)
        o_side = jnp.concatenate([o_meta, o_dec[:, 0],
                                  jnp.zeros((SIDE_ROWS - N_META - db, d_att), F32)], axis=0)

        u, cg, q, k, v, ag, kb, vb = _inproj(hp, norm_w, w_in_bf, l, ROW_TILE)
        z = _conv(u, us, *conv_p, l, batch, seq)
        o = _flash(q, kb, vb, kbs, vbs, bmeta, bnear, lam, l, batch, seq)
        hp = _out_proj(z, cg, o, ag, hp, w_pw2_bf, subln_w, w_out_bf, l, lam_inits[l], OUT_TILE, fin)
        side = _out_proj(z_side, cgs, o_side, ags, side, w_pw2_bf, subln_w, w_out_bf, l,
                         lam_inits[l], SIDE_ROWS, fin)

        k_l.append(k)
        ksd_l.append(ks)
        v_l.append(v)
        vsd_l.append(vs)
        cp_l.append(u.reshape(batch, seq, c_conv)[:, seq - (CONV_K - 1):])
        ks_l.append(ks[N_META:N_META + db])
        vs_l.append(vs.reshape(SIDE_ROWS, N_HEADS, HEAD_DV)[N_META:N_META + db])
        cs_l.append(new_state)

    lp = seq + N_META
    kt = _key_output(k_l, ksd_l, batch, seq)
    new_k = jnp.transpose(kt.reshape(depth, batch, N_HEADS, 2, HEAD_DK, lp), (0, 1, 5, 2, 3, 4))
    new_v = _value_output(v_l, vsd_l, batch, seq)
    return (hp.reshape(batch, seq, d),
            side[N_META:N_META + db].reshape(db, 1, d),
            new_k,
            new_v.reshape(depth, batch, lp, N_HEADS, HEAD_DV),
            jnp.stack(cp_l),
            jnp.stack(ks_l).reshape(depth, db, 1, N_HEADS, 2, HEAD_DK),
            jnp.stack(vs_l).reshape(depth, db, 1, N_HEADS, HEAD_DV),
            jnp.stack(cs_l))
```
